```python
import math
import jax, jax.numpy as jnp
from jax import lax
import numpy as np

D_MODEL = 1024
BATCH = 8
SEQ = 4096
DEPTH = 4

CHUNK = 64
D_SSM = D_MODEL // 2
SSM_GROUP = 16
N_SSM_GROUPS = D_SSM // SSM_GROUP
SSM_STATE = 64
D_SGU = D_MODEL - D_SSM
SGU_HEADS = 8
SGU_HEAD_DIM = D_SGU // SGU_HEADS
SGU_BLOCK = 128
D_IN = D_SSM + 2 * D_SGU
D_FF = -(-8 * D_MODEL // (3 * 256)) * 256
PLE_DIM = 256
ALPHA = (2 * DEPTH) ** 0.25
BETA = (8 * DEPTH) ** -0.25
LN_EPS = 1e-5
DT_MIN, DT_MAX = 1e-3, 1e-1

kernel_name = "hybrid_s5_sgu_deepnorm_encoder"


def layer_norm(x, g, b):
    xf = x.astype(jnp.float32)
    mu = jnp.mean(xf, -1, keepdims=True)
    var = jnp.mean(jnp.square(xf - mu), -1, keepdims=True)
    return ((xf - mu) * lax.rsqrt(var + LN_EPS) * g.astype(jnp.float32)
            + b.astype(jnp.float32)).astype(x.dtype)


def rms_norm(x, g):
    xf = x.astype(jnp.float32)
    return (xf * lax.rsqrt(jnp.mean(xf * xf, -1, keepdims=True) + LN_EPS)
            * g.astype(jnp.float32)).astype(x.dtype)


def s5_mixer(u, a_re, a_im, log_dt, b_re, b_im, c_re, c_im, d_skip, w_glu):
    bsz, seq, _ = u.shape
    f32 = jnp.float32
    ug = u.astype(f32).reshape(bsz, seq, N_SSM_GROUPS, SSM_GROUP)
    lam = lax.complex(a_re.astype(f32), a_im.astype(f32))
    dt = jnp.exp(log_dt.astype(f32))[:, None]
    a_bar = jnp.exp(lam * dt)
    b_cplx = lax.complex(b_re.astype(f32), b_im.astype(f32))
    b_bar = ((a_bar - 1.0) / lam)[..., None] * b_cplx
    bu = jnp.einsum('gph,blgh->blgp', b_bar, ug.astype(jnp.complex64))
    a_seq = jnp.broadcast_to(a_bar[None, None], (1, seq) + a_bar.shape)

    def combine(left, right):
        a_l, b_l = left
        a_r, b_r = right
        return a_l * a_r, a_r * b_l + b_r

    _, states = lax.associative_scan(combine, (a_seq, bu), axis=1)
    c_cplx = lax.complex(c_re.astype(f32), c_im.astype(f32))
    y = jnp.einsum('ghp,blgp->blgh', c_cplx, states).real + d_skip.astype(f32) * ug
    y = jax.nn.gelu(y.reshape(bsz, seq, D_SSM))
    y = y * jax.nn.sigmoid(y @ w_glu.astype(f32))
    return y.astype(u.dtype)


def sgu_mixer(z, ln_g, ln_b, w_s, b_s):
    z = jax.nn.gelu(z)
    u, v = z[..., :D_SGU], z[..., D_SGU:]
    v = layer_norm(v, ln_g, ln_b)
    bsz, seq, _ = v.shape
    n_blk = seq // SGU_BLOCK
    vb = v.reshape(bsz, n_blk, SGU_BLOCK, SGU_HEADS, SGU_HEAD_DIM)
    chunk_id = jnp.arange(SGU_BLOCK) // CHUNK
    mask = chunk_id[:, None] >= chunk_id[None, :]
    w = jnp.where(mask[None], w_s, jnp.zeros_like(w_s))
    s = jnp.einsum('hij,bnjhc->bnihc', w, vb) + b_s.T[None, None, :, :, None]
    return u * s.reshape(bsz, seq, D_SGU)


def setup_inputs(seed: int = 0) -> dict:
    key = jax.random.key(seed)
    ks = iter(jax.random.split(key, 40))

    def nrm(shape, scale):
        return scale * jax.random.normal(next(ks), shape, jnp.float32)

    n_idx = jnp.arange(SSM_STATE, dtype=jnp.float32)
    G, P, H = N_SSM_GROUPS, SSM_STATE, SSM_GROUP
    inp = {}
    inp["x"] = nrm((BATCH, SEQ, D_MODEL), 1.0)
    inp["p"] = nrm((DEPTH, BATCH, SEQ, PLE_DIM), 1.0)
    inp["emb_ln_g"] = 1.0 + nrm((D_MODEL,), 0.02)
    inp["emb_ln_b"] = nrm((D_MODEL,), 0.02)
    inp["w_in"] = nrm((DEPTH, D_MODEL, D_IN), D_MODEL ** -0.5)
    inp["ssm_a_re"] = -0.5 + nrm((DEPTH, G, P), 0.01)
    inp["ssm_a_im"] = math.pi * n_idx + nrm((DEPTH, G, P), 0.01)
    inp["ssm_log_dt"] = jax.random.uniform(next(ks), (DEPTH, G), jnp.float32,
                                           math.log(DT_MIN), math.log(DT_MAX))
    inp["ssm_b_re"] = nrm((DEPTH, G, P, H), (2 * H) ** -0.5)
    inp["ssm_b_im"] = nrm((DEPTH, G, P, H), (2 * H) ** -0.5)
    inp["ssm_c_re"] = nrm((DEPTH, G, H, P), (2 * P) ** -0.5)
    inp["ssm_c_im"] = nrm((DEPTH, G, H, P), (2 * P) ** -0.5)
    inp["ssm_d"] = nrm((DEPTH, G, H), 1.0)
    inp["ssm_w_glu"] = nrm((DEPTH, D_SSM, D_SSM), D_SSM ** -0.5)
    inp["sgu_ln_g"] = 1.0 + nrm((DEPTH, D_SGU), 0.02)
    inp["sgu_ln_b"] = nrm((DEPTH, D_SGU), 0.02)
    inp["sgu_w_s"] = nrm((DEPTH, SGU_HEADS, SGU_BLOCK, SGU_BLOCK), SGU_BLOCK ** -0.5)
    inp["sgu_b_s"] = 1.0 + nrm((DEPTH, SGU_HEADS, SGU_BLOCK), 0.1)
    inp["out_g_ssm"] = 1.0 + nrm((DEPTH, D_SSM), 0.02)
    inp["out_g_sgu"] = 1.0 + nrm((DEPTH, D_SGU), 0.02)
    inp["w_out"] = nrm((DEPTH, D_MODEL, D_MODEL), BETA * D_MODEL ** -0.5)
    inp["ln1_g"] = 1.0 + nrm((DEPTH, D_MODEL), 0.02)
    inp["ln1_b"] = nrm((DEPTH, D_MODEL), 0.02)
    inp["w_ffn_gate"] = nrm((DEPTH, D_MODEL, D_FF), D_MODEL ** -0.5)
    inp["w_ffn_up"] = nrm((DEPTH, D_MODEL, D_FF), D_MODEL ** -0.5)
    inp["w_ffn_down"] = nrm((DEPTH, D_FF, D_MODEL), BETA * D_FF ** -0.5)
    inp["w_ple"] = nrm((DEPTH, PLE_DIM, D_MODEL), BETA * PLE_DIM ** -0.5)
    inp["w_ple_gate"] = nrm((DEPTH, D_MODEL, D_MODEL), D_MODEL ** -0.5)
    inp["b_ple_gate"] = nrm((DEPTH, D_MODEL), 0.02)
    inp["ln2_g"] = 1.0 + nrm((DEPTH, D_MODEL), 0.02)
    inp["ln2_b"] = nrm((DEPTH, D_MODEL), 0.02)
    return inp


def reference(x, p, emb_ln_g, emb_ln_b, w_in, ssm_a_re, ssm_a_im, ssm_log_dt,
              ssm_b_re, ssm_b_im, ssm_c_re, ssm_c_im, ssm_d, ssm_w_glu,
              sgu_ln_g, sgu_ln_b, sgu_w_s, sgu_b_s, out_g_ssm, out_g_sgu, w_out,
              ln1_g, ln1_b, w_ffn_gate, w_ffn_up, w_ffn_down, w_ple, w_ple_gate,
              b_ple_gate, ln2_g, ln2_b):
    h = layer_norm(x, emb_ln_g, emb_ln_b)
    for i in range(DEPTH):
        z = h @ w_in[i]
        y_ssm = s5_mixer(z[..., :D_SSM], ssm_a_re[i], ssm_a_im[i], ssm_log_dt[i],
                         ssm_b_re[i], ssm_b_im[i], ssm_c_re[i], ssm_c_im[i],
                         ssm_d[i], ssm_w_glu[i])
        y_sgu = sgu_mixer(z[..., D_SSM:], sgu_ln_g[i], sgu_ln_b[i], sgu_w_s[i], sgu_b_s[i])
        mix = jnp.concatenate([rms_norm(y_ssm, out_g_ssm[i]),
                               rms_norm(y_sgu, out_g_sgu[i])], axis=-1) @ w_out[i]
        h = layer_norm(ALPHA * h + mix, ln1_g[i], ln1_b[i])
        ffn = (jax.nn.silu(h @ w_ffn_gate[i]) * (h @ w_ffn_up[i])) @ w_ffn_down[i]
        ple = (p[i] @ w_ple[i]) * jax.nn.sigmoid(h @ w_ple_gate[i] + b_ple_gate[i])
        h = layer_norm(ALPHA * h + ffn + ple, ln2_g[i], ln2_b[i])
    return h
```

```python
import functools
import math

import jax
import jax.numpy as jnp
from jax import lax
from jax.experimental import pallas as pl
from jax.experimental.pallas import tpu as pltpu

F32 = jnp.float32
BF16 = jnp.bfloat16

LN_EPS = 1e-5
CHUNK = 64
SGU_BLOCK = 128
SGU_HEADS = 8
SSM_GROUP = 16
SSM_STATE = 64
T_TILE = SGU_BLOCK
SUBLANES = 8
LANES = 128
SCAN_W = 512
FFN_ROWS = 512
FFN_CHUNK = 512
VMEM_LIMIT = 56 * 1024 * 1024


def _gelu(x):
    c = math.sqrt(2.0 / math.pi)
    return 0.5 * x * (1.0 + jnp.tanh(c * (x + 0.044715 * (x * x * x))))


def _sigmoid(x):
    return 1.0 / (1.0 + jnp.exp(-x))


def _layer_norm(x, g, b):
    mu = jnp.mean(x, axis=-1, keepdims=True)
    xc = x - mu
    var = jnp.mean(xc * xc, axis=-1, keepdims=True)
    return xc * lax.rsqrt(var + LN_EPS) * g + b


def _rms_norm(x, g):
    return x * lax.rsqrt(jnp.mean(x * x, axis=-1, keepdims=True) + LN_EPS) * g


def _s5_prep_kernel(are_ref, aim_ref, ldt_ref, bre_ref, bim_ref,
                    abr_ref, abi_ref, bbr_ref, bbi_ref):
    a_re = are_ref[...]
    a_im = aim_ref[...]
    dt = jnp.exp(ldt_ref[...])
    mag = jnp.exp(a_re * dt)
    ab_re = mag * jnp.cos(a_im * dt)
    ab_im = mag * jnp.sin(a_im * dt)
    x_re = ab_re - 1.0
    den = a_re * a_re + a_im * a_im
    q_re = (x_re * a_re + ab_im * a_im) / den
    q_im = (ab_im * a_re - x_re * a_im) / den
    b_re = bre_ref[...]
    b_im = bim_ref[...]
    abr_ref[...] = ab_re
    abi_ref[...] = ab_im
    bbr_ref[...] = q_re * b_re - q_im * b_im
    bbi_ref[...] = q_re * b_im + q_im * b_re


def _s5_prep(a_re, a_im, log_dt, b_re, b_im):
    depth, g, p, h = b_re.shape
    shp = (depth * g, p * h)
    bc = lambda x: jnp.broadcast_to(x, (depth, g, p, h)).reshape(shp)
    ins = (bc(a_re[..., None]), bc(a_im[..., None]), bc(log_dt[..., None, None]),
           b_re.reshape(shp), b_im.reshape(shp))
    outs = pl.pallas_call(
        _s5_prep_kernel,
        out_shape=[jax.ShapeDtypeStruct(shp, F32)] * 4,
        name="s5_prep",
    )(*ins)
    abr, abi, bbr, bbi = (o.reshape(depth, g, p, h) for o in outs)
    return abr[..., 0], abi[..., 0], bbr, bbi


def _mixer_kernel(h_ref, win_ref, bfull_ref, cfull_ref, are_ref, aim_ref, dskip_ref,
                  wglu_ref, sgug_ref, sgub_ref, wsgu_ref, bsgu_ref, gssm_ref, gsgu_ref,
                  wout_ref, ln1g_ref, ln1b_ref, out_ref,
                  zt_ref, bu_ref, yt_ref, state_ref, *, alpha):
    nb, tt, d = h_ref.shape
    rows = nb * tt
    d_ssm = dskip_ref.shape[-1]
    n_sb = bfull_ref.shape[0]
    k_sb = bfull_ref.shape[1]
    n_cplx = bfull_ref.shape[2] // 2
    n_slab = d_ssm // LANES

    @pl.when(pl.program_id(0) == 0)
    def _():
        state_ref[...] = jnp.zeros_like(state_ref)

    h = h_ref[...].reshape(rows, d)
    z = jnp.dot(h.astype(BF16), win_ref[...], preferred_element_type=F32)
    z_ssm = z[:, :d_ssm]

    for b in range(nb):
        for k in range(n_slab):
            zt_ref[k, pl.ds(b, tt, stride=nb), :] = (
                z_ssm[b * tt:(b + 1) * tt, k * LANES:(k + 1) * LANES])

    slabs_per_sb = k_sb // LANES
    for j in range(n_sb):
        zj = jnp.concatenate(
            [zt_ref[j * slabs_per_sb + k] for k in range(slabs_per_sb)], axis=-1)
        bu_ref[...] = jnp.dot(zj.astype(BF16), bfull_ref[j], preferred_element_type=F32)
        for lo in range(0, n_cplx, SCAN_W):
            re_sl = slice(lo, lo + SCAN_W)
            im_sl = slice(n_cplx + lo, n_cplx + lo + SCAN_W)
            a_re = are_ref[j, :, re_sl]
            a_im = aim_ref[j, :, re_sl]

            def step(t, carry, re_sl=re_sl, im_sl=im_sl, a_re=a_re, a_im=a_im):
                s_re, s_im = carry
                r = pl.multiple_of(t * nb, nb)
                n_re = a_re * s_re - a_im * s_im + bu_ref[pl.ds(r, nb), re_sl]
                n_im = a_re * s_im + a_im * s_re + bu_ref[pl.ds(r, nb), im_sl]
                bu_ref[pl.ds(r, nb), re_sl] = n_re
                bu_ref[pl.ds(r, nb), im_sl] = n_im
                return n_re, n_im

            s_re, s_im = lax.fori_loop(
                0, tt, step, (state_ref[j, :, re_sl], state_ref[j, :, im_sl]), unroll=8)
            state_ref[j, :, re_sl] = s_re
            state_ref[j, :, im_sl] = s_im
        yj = jnp.dot(bu_ref[...].astype(BF16), cfull_ref[j], preferred_element_type=F32)
        for k in range(slabs_per_sb):
            yt_ref[j * slabs_per_sb + k] = yj[:, k * LANES:(k + 1) * LANES]

    y_lin = jnp.concatenate(
        [jnp.concatenate([yt_ref[k, pl.ds(b, tt, stride=nb), :] for k in range(n_slab)],
                         axis=-1) for b in range(nb)], axis=0)
    y = _gelu(y_lin + dskip_ref[...] * z_ssm)
    gate = jnp.dot(y.astype(BF16), wglu_ref[...], preferred_element_type=F32)
    y_ssm = y * _sigmoid(gate)

    d_sgu = sgug_ref.shape[-1]
    zs = _gelu(z[:, d_ssm:])
    u = zs[:, :d_sgu]
    v = _layer_norm(zs[:, d_sgu:], sgug_ref[...], sgub_ref[...])
    head_dim = d_sgu // SGU_HEADS
    n_pair = wsgu_ref.shape[0]
    row_chunk = lax.broadcasted_iota(jnp.int32, (tt, 2 * tt), 0) // CHUNK
    col_chunk = (lax.broadcasted_iota(jnp.int32, (tt, 2 * tt), 1) % tt) // CHUNK
    causal = row_chunk >= col_chunk
    w_pairs = [jnp.where(causal, wsgu_ref[q], 0.0).astype(BF16) for q in range(n_pair)]
    lane = lax.broadcasted_iota(jnp.int32, (tt, 2 * head_dim), 1)
    first_head = lane < head_dim
    s_rows = []
    for b in range(nb):
        vb = v[b * tt:(b + 1) * tt]
        parts = []
        for q in range(n_pair):
            vp = vb[:, q * 2 * head_dim:(q + 1) * 2 * head_dim]
            rhs = jnp.concatenate(
                [jnp.where(first_head, vp, 0.0), jnp.where(first_head, 0.0, vp)], axis=0)
            parts.append(jnp.dot(w_pairs[q], rhs.astype(BF16), preferred_element_type=F32))
        s_rows.append(jnp.concatenate(parts, axis=-1) + bsgu_ref[...])
    y_sgu = u * jnp.concatenate(s_rows, axis=0)

    n_ssm = _rms_norm(y_ssm, gssm_ref[...]).astype(BF16)
    n_sgu = _rms_norm(y_sgu, gsgu_ref[...]).astype(BF16)
    mix = (jnp.dot(n_ssm, wout_ref[:d_ssm, :], preferred_element_type=F32)
           + jnp.dot(n_sgu, wout_ref[d_ssm:, :], preferred_element_type=F32))
    out = _layer_norm(alpha * h + mix, ln1g_ref[...], ln1b_ref[...])
    out_ref[...] = out.reshape(nb, tt, d)


def _const_spec(shape):
    zeros = (0,) * len(shape)
    return pl.BlockSpec(shape, lambda *_: zeros, pipeline_mode=pl.Buffered(1))


def _mixer_call(h, w, alpha):
    nb, seq, d = h.shape
    d_ssm = w["dskip"].shape[-1]
    n_sb, k_sb, n_state = w["bfull"].shape
    consts = [w["win"], w["bfull"], w["cfull"], w["are"], w["aim"], w["dskip"], w["wglu"],
              w["sgug"], w["sgub"], w["wsgu"], w["bsgu"], w["gssm"], w["gsgu"], w["wout"],
              w["ln1g"], w["ln1b"]]
    rows = nb * T_TILE
    return pl.pallas_call(
        functools.partial(_mixer_kernel, alpha=alpha),
        grid=(seq // T_TILE,),
        in_specs=[pl.BlockSpec((nb, T_TILE, d), lambda c: (0, c, 0))]
                 + [_const_spec(x.shape) for x in consts],
        out_specs=pl.BlockSpec((nb, T_TILE, d), lambda c: (0, c, 0)),
        out_shape=jax.ShapeDtypeStruct(h.shape, F32),
        scratch_shapes=[
            pltpu.VMEM((d_ssm // LANES, rows, LANES), F32),
            pltpu.VMEM((rows, n_state), F32),
            pltpu.VMEM((d_ssm // LANES, rows, LANES), F32),
            pltpu.VMEM((n_sb, nb, n_state), F32),
        ],
        compiler_params=pltpu.CompilerParams(
            dimension_semantics=("arbitrary",), vmem_limit_bytes=VMEM_LIMIT),
        name="mixer",
    )(h, *consts)


def _ffn_kernel(h_ref, p_ref, wg_ref, wu_ref, wd_ref, wple_ref, wpg_ref, bpg_ref,
                ln2g_ref, ln2b_ref, out_ref, *, alpha):
    h = h_ref[...]
    hb = h.astype(BF16)
    d_ff = wg_ref.shape[1]
    acc = None
    for off in range(0, d_ff, FFN_CHUNK):
        sz = min(FFN_CHUNK, d_ff - off)
        g = jnp.dot(hb, wg_ref[:, off:off + sz], preferred_element_type=F32)
        u = jnp.dot(hb, wu_ref[:, off:off + sz], preferred_element_type=F32)
        a = (g * _sigmoid(g) * u).astype(BF16)
        part = jnp.dot(a, wd_ref[off:off + sz, :], preferred_element_type=F32)
        acc = part if acc is None else acc + part
    ple_lin = jnp.dot(p_ref[...].astype(BF16), wple_ref[...], preferred_element_type=F32)
    ple_gate = jnp.dot(hb, wpg_ref[...], preferred_element_type=F32) + bpg_ref[...]
    ple = ple_lin * _sigmoid(ple_gate)
    out_ref[...] = _layer_norm(alpha * h + acc + ple, ln2g_ref[...], ln2b_ref[...])


def _ffn_call(h2, p2, w, alpha):
    m, d = h2.shape
    consts = [w["wg"], w["wu"], w["wd"], w["wple"], w["wpg"], w["bpg"], w["ln2g"], w["ln2b"]]
    return pl.pallas_call(
        functools.partial(_ffn_kernel, alpha=alpha),
        grid=(m // FFN_ROWS,),
        in_specs=[pl.BlockSpec((FFN_ROWS, d), lambda i: (i, 0)),
                  pl.BlockSpec((FFN_ROWS, p2.shape[1]), lambda i: (i, 0))]
                 + [_const_spec(x.shape) for x in consts],
        out_specs=pl.BlockSpec((FFN_ROWS, d), lambda i: (i, 0)),
        out_shape=jax.ShapeDtypeStruct(h2.shape, F32),
        compiler_params=pltpu.CompilerParams(
            dimension_semantics=("parallel",), vmem_limit_bytes=VMEM_LIMIT),
        name="ffn",
    )(h2, p2, *consts)


def _emb_ln_kernel(x_ref, g_ref, b_ref, out_ref):
    out_ref[...] = _layer_norm(x_ref[...], g_ref[...], b_ref[...])


def _emb_ln(x2, g, b):
    m, d = x2.shape
    rows = 1024
    return pl.pallas_call(
        _emb_ln_kernel,
        grid=(m // rows,),
        in_specs=[pl.BlockSpec((rows, d), lambda i: (i, 0)),
                  pl.BlockSpec((1, d), lambda i: (0, 0)),
                  pl.BlockSpec((1, d), lambda i: (0, 0))],
        out_specs=pl.BlockSpec((rows, d), lambda i: (i, 0)),
        out_shape=jax.ShapeDtypeStruct(x2.shape, F32),
        compiler_params=pltpu.CompilerParams(dimension_semantics=("parallel",)),
        name="emb_ln",
    )(x2, g.reshape(1, d), b.reshape(1, d))


def _ssm_layout(abr, abi, bbr, bbi, c_re, c_im, n_sb):
    g, p, hh = bbr.shape
    gl = g // n_sb
    eye = jnp.eye(gl, dtype=F32)

    def b_mat(x):
        x = x.reshape(n_sb, gl, p, hh)
        return jnp.einsum("jgph,gk->jghkp", x, eye).reshape(n_sb, gl * hh, gl * p)

    def c_mat(x):
        x = x.reshape(n_sb, gl, hh, p)
        return jnp.einsum("jghp,gk->jgpkh", x, eye).reshape(n_sb, gl * p, gl * hh)

    bfull = jnp.concatenate([b_mat(bbr), b_mat(bbi)], axis=-1).astype(BF16)
    cfull = jnp.concatenate([c_mat(c_re), -c_mat(c_im)], axis=-2).astype(BF16)
    a_cols = lambda a: jnp.broadcast_to(
        a.reshape(n_sb, 1, gl * p), (n_sb, SUBLANES, gl * p)).astype(F32)
    return bfull, cfull, a_cols(abr), a_cols(abi)


def kernel(x, p, emb_ln_g, emb_ln_b, w_in, ssm_a_re, ssm_a_im, ssm_log_dt, ssm_b_re, ssm_b_im, ssm_c_re, ssm_c_im, ssm_d, ssm_w_glu, sgu_ln_g, sgu_ln_b, sgu_w_s, sgu_b_s, out_g_ssm, out_g_sgu, w_out, ln1_g, ln1_b, w_ffn_gate, w_ffn_up, w_ffn_down, w_ple, w_ple_gate, b_ple_gate, ln2_g, ln2_b):
    bsz, seq, d = x.shape
    depth = w_in.shape[0]
    assert bsz == SUBLANES and seq % T_TILE == 0
    alpha = (2 * depth) ** 0.25
    d_ssm = ssm_w_glu.shape[-1]
    d_sgu = sgu_ln_g.shape[-1]
    head_dim = d_sgu // SGU_HEADS
    n_sb = 2
    row = lambda a: a.reshape(1, -1).astype(F32)

    abr, abi, bbr, bbi = _s5_prep(ssm_a_re, ssm_a_im, ssm_log_dt, ssm_b_re, ssm_b_im)

    h = _emb_ln(x.reshape(bsz * seq, d), emb_ln_g, emb_ln_b)
    for i in range(depth):
        bfull, cfull, are, aim = _ssm_layout(abr[i], abi[i], bbr[i], bbi[i],
                                             ssm_c_re[i], ssm_c_im[i], n_sb)
        ws = sgu_w_s[i]
        wsgu = jnp.concatenate([ws[0::2], ws[1::2]], axis=-1)
        bsgu = jnp.repeat(sgu_b_s[i].T, head_dim, axis=1)
        wm = dict(
            win=w_in[i].astype(BF16), bfull=bfull, cfull=cfull, are=are, aim=aim,
            dskip=row(ssm_d[i]), wglu=ssm_w_glu[i].astype(BF16),
            sgug=row(sgu_ln_g[i]), sgub=row(sgu_ln_b[i]), wsgu=wsgu, bsgu=bsgu,
            gssm=row(out_g_ssm[i]), gsgu=row(out_g_sgu[i]), wout=w_out[i].astype(BF16),
            ln1g=row(ln1_g[i]), ln1b=row(ln1_b[i]))
        h = _mixer_call(h.reshape(bsz, seq, d), wm, alpha)
        wf = dict(
            wg=w_ffn_gate[i].astype(BF16), wu=w_ffn_up[i].astype(BF16),
            wd=w_ffn_down[i].astype(BF16), wple=w_ple[i].astype(BF16),
            wpg=w_ple_gate[i].astype(BF16), bpg=row(b_ple_gate[i]),
            ln2g=row(ln2_g[i]), ln2b=row(ln2_b[i]))
        h = _ffn_call(h.reshape(bsz * seq, d), p[i].reshape(bsz * seq, -1), wf, alpha)
    return h.reshape(bsz, seq, d)
```

```python
import functools
import math

import jax
import jax.numpy as jnp
from jax import lax
from jax.experimental import pallas as pl
from jax.experimental.pallas import tpu as pltpu

F32 = jnp.float32
BF16 = jnp.bfloat16

LN_EPS = 1e-5
CHUNK = 64
SGU_BLOCK = 128
SGU_HEADS = 8
SGU_HEAD_GROUP = 4
T_TILE = SGU_BLOCK
T_SUB = CHUNK
SUBLANES = 8
LANES = 128
SCAN_W = 512
FFN_ROWS = 512
FFN_CHUNK = 512
VMEM_LIMIT = 56 * 1024 * 1024


def _gelu(x):
    c = math.sqrt(2.0 / math.pi)
    return 0.5 * x * (1.0 + jnp.tanh(c * (x + 0.044715 * (x * x * x))))


def _sigmoid(x):
    return 1.0 / (1.0 + jnp.exp(-x))


def _layer_norm(x, g, b):
    mu = jnp.mean(x, axis=-1, keepdims=True)
    xc = x - mu
    var = jnp.mean(xc * xc, axis=-1, keepdims=True)
    return xc * lax.rsqrt(var + LN_EPS) * g + b


def _rms_norm(x, g):
    return x * lax.rsqrt(jnp.mean(x * x, axis=-1, keepdims=True) + LN_EPS) * g


def _s5_prep_kernel(are_ref, aim_ref, ldt_ref, bre_ref, bim_ref,
                    abr_ref, abi_ref, bbr_ref, bbi_ref):
    a_re = are_ref[...]
    a_im = aim_ref[...]
    dt = jnp.exp(ldt_ref[...])
    mag = jnp.exp(a_re * dt)
    ab_re = mag * jnp.cos(a_im * dt)
    ab_im = mag * jnp.sin(a_im * dt)
    x_re = ab_re - 1.0
    den = a_re * a_re + a_im * a_im
    q_re = (x_re * a_re + ab_im * a_im) / den
    q_im = (ab_im * a_re - x_re * a_im) / den
    b_re = bre_ref[...]
    b_im = bim_ref[...]
    abr_ref[...] = ab_re
    abi_ref[...] = ab_im
    bbr_ref[...] = q_re * b_re - q_im * b_im
    bbi_ref[...] = q_re * b_im + q_im * b_re


def _s5_prep(a_re, a_im, log_dt, b_re, b_im):
    depth, g, p, h = b_re.shape
    shp = (depth * g, p * h)
    bc = lambda x: jnp.broadcast_to(x, (depth, g, p, h)).reshape(shp)
    ins = (bc(a_re[..., None]), bc(a_im[..., None]), bc(log_dt[..., None, None]),
           b_re.reshape(shp), b_im.reshape(shp))
    outs = pl.pallas_call(
        _s5_prep_kernel,
        out_shape=[jax.ShapeDtypeStruct(shp, F32)] * 4,
        name="s5_prep",
    )(*ins)
    abr, abi, bbr, bbi = (o.reshape(depth, g, p, h) for o in outs)
    return abr[..., 0], abi[..., 0], bbr, bbi


def _mixer_kernel(*refs, alpha, n_sub):
    (h_ref, win_ref, bfull_ref, cfull_ref, are_ref, aim_ref, dskip_ref, wglu_ref,
     sgug_ref, sgub_ref) = refs[:10]
    wsgu_refs = refs[10:10 + n_sub]
    (bsgu_ref, gssm_ref, gsgu_ref, wout_ref, ln1g_ref, ln1b_ref, out_ref,
     zt_ref, bu_ref, sb_ref, yt_ref, state_ref) = refs[10 + n_sub:]

    nb, tt, d = h_ref.shape
    ts = tt // n_sub
    rows = nb * ts
    d_ssm = dskip_ref.shape[-1]
    d_sgu = sgug_ref.shape[-1]
    n_sb = bfull_ref.shape[0]
    k_sb = bfull_ref.shape[1]
    n_cplx = bfull_ref.shape[2] // 2
    n_slab = d_ssm // LANES
    slabs_per_sb = k_sb // LANES
    head_dim = d_sgu // SGU_HEADS
    n_hgrp = wsgu_refs[0].shape[0]
    grp_w = SGU_HEAD_GROUP * head_dim
    lane_head = lax.broadcasted_iota(jnp.int32, (ts, grp_w), 1) // head_dim

    @pl.when(pl.program_id(0) == 0)
    def _():
        state_ref[...] = jnp.zeros_like(state_ref)

    chains = [(j, lo) for j in range(n_sb) for lo in range(0, n_cplx, SCAN_W)]
    state = {(j, lo): (state_ref[j, :, lo:lo + SCAN_W],
                       state_ref[j, :, n_cplx + lo:n_cplx + lo + SCAN_W]) for j, lo in chains}
    sub = [dict() for _ in range(n_sub)]

    def st_in(s):
        v = sub[s]
        v["h"] = h_ref[:, s * ts:(s + 1) * ts, :].reshape(rows, d)
        v["z"] = jnp.dot(v["h"].astype(BF16), win_ref[...], preferred_element_type=F32)

    def st_zt(s):
        z_ssm = sub[s]["z"][:, :d_ssm]
        for b in range(nb):
            for k in range(n_slab):
                zt_ref[s, k, pl.ds(b, ts, stride=nb), :] = (
                    z_ssm[b * ts:(b + 1) * ts, k * LANES:(k + 1) * LANES])

    def st_b(s):
        for j in range(n_sb):
            zj = jnp.concatenate(
                [zt_ref[s, j * slabs_per_sb + k] for k in range(slabs_per_sb)], axis=-1)
            bu_ref[s, j] = jnp.dot(zj.astype(BF16), bfull_ref[j], preferred_element_type=F32)

    def st_scan(s):
        for j, lo in chains:
            re_sl = slice(lo, lo + SCAN_W)
            im_sl = slice(n_cplx + lo, n_cplx + lo + SCAN_W)
            a_re = are_ref[j, :, re_sl]
            a_im = aim_ref[j, :, re_sl]
            s_re, s_im = state[(j, lo)]
            for t in range(0, ts, 2):
                r0 = slice(t * nb, (t + 1) * nb)
                r1 = slice((t + 1) * nb, (t + 2) * nb)
                m_re = a_re * s_re - a_im * s_im + bu_ref[s, j, r0, re_sl]
                m_im = a_re * s_im + a_im * s_re + bu_ref[s, j, r0, im_sl]
                s_re = a_re * m_re - a_im * m_im + bu_ref[s, j, r1, re_sl]
                s_im = a_re * m_im + a_im * m_re + bu_ref[s, j, r1, im_sl]
                r01 = slice(t * nb, (t + 2) * nb)
                sb_ref[s, j, r01, re_sl] = jnp.concatenate([m_re, s_re], axis=0).astype(BF16)
                sb_ref[s, j, r01, im_sl] = jnp.concatenate([m_im, s_im], axis=0).astype(BF16)
            state[(j, lo)] = (s_re, s_im)

    def st_c(s):
        for j in range(n_sb):
            yj = jnp.dot(sb_ref[s, j], cfull_ref[j], preferred_element_type=F32)
            for k in range(slabs_per_sb):
                yt_ref[s, j * slabs_per_sb + k] = yj[:, k * LANES:(k + 1) * LANES]
        y_lin = jnp.concatenate(
            [jnp.concatenate([yt_ref[s, k, pl.ds(b, ts, stride=nb), :] for k in range(n_slab)],
                             axis=-1) for b in range(nb)], axis=0)
        sub[s]["y"] = _gelu(y_lin + dskip_ref[...] * sub[s]["z"][:, :d_ssm])

    def st_glu(s):
        y = sub[s]["y"]
        gate = jnp.dot(y.astype(BF16), wglu_ref[...], preferred_element_type=F32)
        sub[s]["n_ssm"] = _rms_norm(y * _sigmoid(gate), gssm_ref[...]).astype(BF16)

    def st_sgu_pre(s):
        zs = _gelu(sub[s]["z"][:, d_ssm:])
        sub[s]["u"] = zs[:, :d_sgu]
        sub[s]["v"] = _layer_norm(zs[:, d_sgu:], sgug_ref[...], sgub_ref[...])

    def st_sgu(s):
        s_rows = []
        for b in range(nb):
            parts = []
            for q in range(n_hgrp):
                rhs = jnp.concatenate(
                    [jnp.where(lane_head == i,
                               sub[c]["v"][b * ts:(b + 1) * ts, q * grp_w:(q + 1) * grp_w], 0.0)
                     for i in range(SGU_HEAD_GROUP) for c in range(s + 1)], axis=0)
                parts.append(jnp.dot(wsgu_refs[s][q], rhs.astype(BF16),
                                     preferred_element_type=F32))
            s_rows.append(jnp.concatenate(parts, axis=-1) + bsgu_ref[s * ts:(s + 1) * ts, :])
        y_sgu = sub[s]["u"] * jnp.concatenate(s_rows, axis=0)
        sub[s]["n_sgu"] = _rms_norm(y_sgu, gsgu_ref[...]).astype(BF16)

    def st_out(s):
        v = sub[s]
        mix = (jnp.dot(v["n_ssm"], wout_ref[:d_ssm, :], preferred_element_type=F32)
               + jnp.dot(v["n_sgu"], wout_ref[d_ssm:, :], preferred_element_type=F32))
        out = _layer_norm(alpha * v["h"] + mix, ln1g_ref[...], ln1b_ref[...])
        out_ref[:, s * ts:(s + 1) * ts, :] = out.reshape(nb, ts, d)

    for stage_group in ((st_in,), (st_zt, st_sgu_pre, st_b), (st_scan, st_c),
                        (st_glu, st_sgu), (st_out,)):
        for s in range(n_sub):
            for stage in stage_group:
                stage(s)

    for j, lo in chains:
        state_ref[j, :, lo:lo + SCAN_W] = state[(j, lo)][0]
        state_ref[j, :, n_cplx + lo:n_cplx + lo + SCAN_W] = state[(j, lo)][1]


def _const_spec(shape):
    zeros = (0,) * len(shape)
    return pl.BlockSpec(shape, lambda *_: zeros, pipeline_mode=pl.Buffered(1))


def _mixer_call(h, w, alpha):
    nb, seq, d = h.shape
    d_ssm = w["dskip"].shape[-1]
    n_sb, k_sb, n_state = w["bfull"].shape
    n_sub = len(w["wsgu"])
    consts = ([w["win"], w["bfull"], w["cfull"], w["are"], w["aim"], w["dskip"], w["wglu"],
               w["sgug"], w["sgub"]] + list(w["wsgu"])
              + [w["bsgu"], w["gssm"], w["gsgu"], w["wout"], w["ln1g"], w["ln1b"]])
    rows = nb * T_TILE // n_sub
    return pl.pallas_call(
        functools.partial(_mixer_kernel, alpha=alpha, n_sub=n_sub),
        grid=(seq // T_TILE,),
        in_specs=[pl.BlockSpec((nb, T_TILE, d), lambda c: (0, c, 0))]
                 + [_const_spec(x.shape) for x in consts],
        out_specs=pl.BlockSpec((nb, T_TILE, d), lambda c: (0, c, 0)),
        out_shape=jax.ShapeDtypeStruct(h.shape, F32),
        scratch_shapes=[
            pltpu.VMEM((n_sub, d_ssm // LANES, rows, LANES), F32),
            pltpu.VMEM((n_sub, n_sb, rows, n_state), F32),
            pltpu.VMEM((n_sub, n_sb, rows, n_state), BF16),
            pltpu.VMEM((n_sub, d_ssm // LANES, rows, LANES), F32),
            pltpu.VMEM((n_sb, nb, n_state), F32),
        ],
        compiler_params=pltpu.CompilerParams(
            dimension_semantics=("arbitrary",), vmem_limit_bytes=VMEM_LIMIT),
        name="mixer",
    )(h, *consts)


def _ffn_kernel(h_ref, p_ref, wg_ref, wu_ref, wd_ref, wple_ref, wpg_ref, bpg_ref,
                ln2g_ref, ln2b_ref, out_ref, *, alpha):
    h = h_ref[...]
    hb = h.astype(BF16)
    d_ff = wg_ref.shape[1]
    acc = None
    for off in range(0, d_ff, FFN_CHUNK):
        sz = min(FFN_CHUNK, d_ff - off)
        g = jnp.dot(hb, wg_ref[:, off:off + sz], preferred_element_type=F32)
        u = jnp.dot(hb, wu_ref[:, off:off + sz], preferred_element_type=F32)
        a = (g * _sigmoid(g) * u).astype(BF16)
        part = jnp.dot(a, wd_ref[off:off + sz, :], preferred_element_type=F32)
        acc = part if acc is None else acc + part
    ple_lin = jnp.dot(p_ref[...].astype(BF16), wple_ref[...], preferred_element_type=F32)
    ple_gate = jnp.dot(hb, wpg_ref[...], preferred_element_type=F32) + bpg_ref[...]
    ple = ple_lin * _sigmoid(ple_gate)
    out_ref[...] = _layer_norm(alpha * h + acc + ple, ln2g_ref[...], ln2b_ref[...])


def _ffn_call(h2, p2, w, alpha):
    m, d = h2.shape
    consts = [w["wg"], w["wu"], w["wd"], w["wple"], w["wpg"], w["bpg"], w["ln2g"], w["ln2b"]]
    return pl.pallas_call(
        functools.partial(_ffn_kernel, alpha=alpha),
        grid=(m // FFN_ROWS,),
        in_specs=[pl.BlockSpec((FFN_ROWS, d), lambda i: (i, 0)),
                  pl.BlockSpec((FFN_ROWS, p2.shape[1]), lambda i: (i, 0))]
                 + [_const_spec(x.shape) for x in consts],
        out_specs=pl.BlockSpec((FFN_ROWS, d), lambda i: (i, 0)),
        out_shape=jax.ShapeDtypeStruct(h2.shape, F32),
        compiler_params=pltpu.CompilerParams(
            dimension_semantics=("parallel",), vmem_limit_bytes=VMEM_LIMIT),
        name="ffn",
    )(h2, p2, *consts)


def _emb_ln_kernel(x_ref, g_ref, b_ref, out_ref):
    out_ref[...] = _layer_norm(x_ref[...], g_ref[...], b_ref[...])


def _emb_ln(x2, g, b):
    m, d = x2.shape
    rows = 1024
    return pl.pallas_call(
        _emb_ln_kernel,
        grid=(m // rows,),
        in_specs=[pl.BlockSpec((rows, d), lambda i: (i, 0)),
                  pl.BlockSpec((1, d), lambda i: (0, 0)),
                  pl.BlockSpec((1, d), lambda i: (0, 0))],
        out_specs=pl.BlockSpec((rows, d), lambda i: (i, 0)),
        out_shape=jax.ShapeDtypeStruct(x2.shape, F32),
        compiler_params=pltpu.CompilerParams(dimension_semantics=("parallel",)),
        name="emb_ln",
    )(x2, g.reshape(1, d), b.reshape(1, d))


def _ssm_layout(abr, abi, bbr, bbi, c_re, c_im, n_sb):
    g, p, hh = bbr.shape
    gl = g // n_sb
    eye = jnp.eye(gl, dtype=F32)

    def b_mat(x):
        x = x.reshape(n_sb, gl, p, hh)
        return jnp.einsum("jgph,gk->jghkp", x, eye).reshape(n_sb, gl * hh, gl * p)

    def c_mat(x):
        x = x.reshape(n_sb, gl, hh, p)
        return jnp.einsum("jghp,gk->jgpkh", x, eye).reshape(n_sb, gl * p, gl * hh)

    bfull = jnp.concatenate([b_mat(bbr), b_mat(bbi)], axis=-1).astype(BF16)
    cfull = jnp.concatenate([c_mat(c_re), -c_mat(c_im)], axis=-2).astype(BF16)
    a_cols = lambda a: jnp.broadcast_to(
        a.reshape(n_sb, 1, gl * p), (n_sb, SUBLANES, gl * p)).astype(F32)
    return bfull, cfull, a_cols(abr), a_cols(abi)


def _sgu_layout(w_s, n_sub):
    heads = w_s.shape[0]
    out = []
    for s in range(n_sub):
        blk = w_s[:, s * T_SUB:(s + 1) * T_SUB, :(s + 1) * T_SUB]
        blk = blk.reshape(heads // SGU_HEAD_GROUP, SGU_HEAD_GROUP, T_SUB, (s + 1) * T_SUB)
        out.append(jnp.transpose(blk, (0, 2, 1, 3)).reshape(
            heads // SGU_HEAD_GROUP, T_SUB, SGU_HEAD_GROUP * (s + 1) * T_SUB).astype(BF16))
    return out


def kernel(x, p, emb_ln_g, emb_ln_b, w_in, ssm_a_re, ssm_a_im, ssm_log_dt, ssm_b_re, ssm_b_im, ssm_c_re, ssm_c_im, ssm_d, ssm_w_glu, sgu_ln_g, sgu_ln_b, sgu_w_s, sgu_b_s, out_g_ssm, out_g_sgu, w_out, ln1_g, ln1_b, w_ffn_gate, w_ffn_up, w_ffn_down, w_ple, w_ple_gate, b_ple_gate, ln2_g, ln2_b):
    bsz, seq, d = x.shape
    depth = w_in.shape[0]
    assert bsz == SUBLANES and seq % T_TILE == 0
    assert sgu_w_s.shape[-1] == T_TILE and T_TILE % T_SUB == 0
    alpha = (2 * depth) ** 0.25
    d_sgu = sgu_ln_g.shape[-1]
    head_dim = d_sgu // SGU_HEADS
    n_sb = 2
    n_sub = T_TILE // T_SUB
    row = lambda a: a.reshape(1, -1).astype(F32)

    abr, abi, bbr, bbi = _s5_prep(ssm_a_re, ssm_a_im, ssm_log_dt, ssm_b_re, ssm_b_im)

    h = _emb_ln(x.reshape(bsz * seq, d), emb_ln_g, emb_ln_b)
    for i in range(depth):
        bfull, cfull, are, aim = _ssm_layout(abr[i], abi[i], bbr[i], bbi[i],
                                             ssm_c_re[i], ssm_c_im[i], n_sb)
        bsgu = jnp.repeat(sgu_b_s[i].T, head_dim, axis=1)
        wm = dict(
            win=w_in[i].astype(BF16), bfull=bfull, cfull=cfull, are=are, aim=aim,
            dskip=row(ssm_d[i]), wglu=ssm_w_glu[i].astype(BF16),
            sgug=row(sgu_ln_g[i]), sgub=row(sgu_ln_b[i]),
            wsgu=_sgu_layout(sgu_w_s[i], n_sub), bsgu=bsgu,
            gssm=row(out_g_ssm[i]), gsgu=row(out_g_sgu[i]), wout=w_out[i].astype(BF16),
            ln1g=row(ln1_g[i]), ln1b=row(ln1_b[i]))
        h = _mixer_call(h.reshape(bsz, seq, d), wm, alpha)
        wf = dict(
            wg=w_ffn_gate[i].astype(BF16), wu=w_ffn_up[i].astype(BF16),
            wd=w_ffn_down[i].astype(BF16), wple=w_ple[i].astype(BF16),
            wpg=w_ple_gate[i].astype(BF16), bpg=row(b_ple_gate[i]),
            ln2g=row(ln2_g[i]), ln2b=row(ln2_b[i]))
        h = _ffn_call(h.reshape(bsz * seq, d), p[i].reshape(bsz * seq, -1), wf, alpha)
    return h.reshape(bsz, seq, d)
```

```python
import functools
import math

import jax
import jax.numpy as jnp
from jax import lax
from jax.experimental import pallas as pl
from jax.experimental.pallas import tpu as pltpu

F32 = jnp.float32
BF16 = jnp.bfloat16

LN_EPS = 1e-5
CHUNK = 64
SGU_BLOCK = 128
SGU_HEADS = 8
SGU_HEAD_GROUP = 4
T_TILE = SGU_BLOCK
T_SUB = CHUNK
SUBLANES = 8
LANES = 128
SCAN_W = 512
FFN_ROWS = 1024
FFN_SUB_ROWS = 512
FFN_CHUNK = 512
VMEM_LIMIT = 56 * 1024 * 1024


def _gelu(x):
    c = math.sqrt(2.0 / math.pi)
    return 0.5 * x * (1.0 + jnp.tanh(c * (x + 0.044715 * (x * x * x))))


def _sigmoid(x):
    return 1.0 / (1.0 + jnp.exp(-x))


def _layer_norm(x, g, b):
    mu = jnp.mean(x, axis=-1, keepdims=True)
    xc = x - mu
    var = jnp.mean(xc * xc, axis=-1, keepdims=True)
    return xc * lax.rsqrt(var + LN_EPS) * g + b


def _rms_norm(x, g):
    return x * lax.rsqrt(jnp.mean(x * x, axis=-1, keepdims=True) + LN_EPS) * g


def _s5_prep_kernel(are_ref, aim_ref, ldt_ref, bre_ref, bim_ref,
                    abr_ref, abi_ref, bbr_ref, bbi_ref):
    a_re = are_ref[...]
    a_im = aim_ref[...]
    dt = jnp.exp(ldt_ref[...])
    mag = jnp.exp(a_re * dt)
    ab_re = mag * jnp.cos(a_im * dt)
    ab_im = mag * jnp.sin(a_im * dt)
    x_re = ab_re - 1.0
    den = a_re * a_re + a_im * a_im
    q_re = (x_re * a_re + ab_im * a_im) / den
    q_im = (ab_im * a_re - x_re * a_im) / den
    b_re = bre_ref[...]
    b_im = bim_ref[...]
    abr_ref[...] = ab_re
    abi_ref[...] = ab_im
    bbr_ref[...] = q_re * b_re - q_im * b_im
    bbi_ref[...] = q_re * b_im + q_im * b_re


def _s5_prep(a_re, a_im, log_dt, b_re, b_im):
    depth, g, p, h = b_re.shape
    shp = (depth * g, p * h)
    bc = lambda x: jnp.broadcast_to(x, (depth, g, p, h)).reshape(shp)
    ins = (bc(a_re[..., None]), bc(a_im[..., None]), bc(log_dt[..., None, None]),
           b_re.reshape(shp), b_im.reshape(shp))
    outs = pl.pallas_call(
        _s5_prep_kernel,
        out_shape=[jax.ShapeDtypeStruct(shp, F32)] * 4,
        name="s5_prep",
    )(*ins)
    abr, abi, bbr, bbi = (o.reshape(depth, g, p, h) for o in outs)
    return abr[..., 0], abi[..., 0], bbr, bbi


def _mixer_kernel(*refs, alpha, n_sub, embed):
    if embed:
        embg_ref, embb_ref = refs[:2]
        refs = refs[2:]
    (h_ref, win_ref, bfull_ref, cfull_ref, are_ref, aim_ref, dskip_ref, wglu_ref,
     sgug_ref, sgub_ref) = refs[:10]
    wsgu_refs = refs[10:10 + n_sub]
    (bsgu_ref, gssm_ref, gsgu_ref, wout_ref, ln1g_ref, ln1b_ref, out_ref,
     zt_ref, bu_ref, sb_ref, yt_ref, state_ref) = refs[10 + n_sub:]

    nb, tt, d = h_ref.shape
    ts = tt // n_sub
    rows = nb * ts
    d_ssm = dskip_ref.shape[-1]
    d_sgu = sgug_ref.shape[-1]
    n_sb = bfull_ref.shape[0]
    k_sb = bfull_ref.shape[1]
    n_cplx = bfull_ref.shape[2] // 2
    n_slab = d_ssm // LANES
    slabs_per_sb = k_sb // LANES
    head_dim = d_sgu // SGU_HEADS
    n_hgrp = wsgu_refs[0].shape[0]
    grp_w = SGU_HEAD_GROUP * head_dim
    lane_head = lax.broadcasted_iota(jnp.int32, (ts, grp_w), 1) // head_dim

    @pl.when(pl.program_id(0) == 0)
    def _():
        state_ref[...] = jnp.zeros_like(state_ref)

    chains = [(j, lo) for j in range(n_sb) for lo in range(0, n_cplx, SCAN_W)]
    state = {(j, lo): (state_ref[j, :, lo:lo + SCAN_W],
                       state_ref[j, :, n_cplx + lo:n_cplx + lo + SCAN_W]) for j, lo in chains}
    sub = [dict() for _ in range(n_sub)]

    def st_in(s):
        v = sub[s]
        v["h"] = h_ref[:, s * ts:(s + 1) * ts, :].reshape(rows, d)
        if embed:
            v["h"] = _layer_norm(v["h"], embg_ref[...], embb_ref[...])
        v["z"] = jnp.dot(v["h"].astype(BF16), win_ref[...], preferred_element_type=F32)

    def st_zt(s):
        z_ssm = sub[s]["z"][:, :d_ssm]
        for b in range(nb):
            for k in range(n_slab):
                zt_ref[s, k, pl.ds(b, ts, stride=nb), :] = (
                    z_ssm[b * ts:(b + 1) * ts, k * LANES:(k + 1) * LANES])

    def st_b(s):
        for j in range(n_sb):
            zj = jnp.concatenate(
                [zt_ref[s, j * slabs_per_sb + k] for k in range(slabs_per_sb)], axis=-1)
            bu_ref[s, j] = jnp.dot(zj.astype(BF16), bfull_ref[j], preferred_element_type=F32)

    def st_scan(s):
        for j, lo in chains:
            re_sl = slice(lo, lo + SCAN_W)
            im_sl = slice(n_cplx + lo, n_cplx + lo + SCAN_W)
            a_re = are_ref[j, :, re_sl]
            a_im = aim_ref[j, :, re_sl]
            s_re, s_im = state[(j, lo)]
            for t in range(0, ts, 2):
                r0 = slice(t * nb, (t + 1) * nb)
                r1 = slice((t + 1) * nb, (t + 2) * nb)
                m_re = a_re * s_re - a_im * s_im + bu_ref[s, j, r0, re_sl]
                m_im = a_re * s_im + a_im * s_re + bu_ref[s, j, r0, im_sl]
                s_re = a_re * m_re - a_im * m_im + bu_ref[s, j, r1, re_sl]
                s_im = a_re * m_im + a_im * m_re + bu_ref[s, j, r1, im_sl]
                r01 = slice(t * nb, (t + 2) * nb)
                sb_ref[s, j, r01, re_sl] = jnp.concatenate([m_re, s_re], axis=0).astype(BF16)
                sb_ref[s, j, r01, im_sl] = jnp.concatenate([m_im, s_im], axis=0).astype(BF16)
            state[(j, lo)] = (s_re, s_im)

    def st_c(s):
        for j in range(n_sb):
            yj = jnp.dot(sb_ref[s, j], cfull_ref[j], preferred_element_type=F32)
            for k in range(slabs_per_sb):
                yt_ref[s, j * slabs_per_sb + k] = yj[:, k * LANES:(k + 1) * LANES]
        y_lin = jnp.concatenate(
            [jnp.concatenate([yt_ref[s, k, pl.ds(b, ts, stride=nb), :] for k in range(n_slab)],
                             axis=-1) for b in range(nb)], axis=0)
        sub[s]["y"] = _gelu(y_lin + dskip_ref[...] * sub[s]["z"][:, :d_ssm])

    def st_glu(s):
        y = sub[s]["y"]
        gate = jnp.dot(y.astype(BF16), wglu_ref[...], preferred_element_type=F32)
        sub[s]["n_ssm"] = _rms_norm(y * _sigmoid(gate), gssm_ref[...]).astype(BF16)

    def st_sgu_pre(s):
        zs = _gelu(sub[s]["z"][:, d_ssm:])
        sub[s]["u"] = zs[:, :d_sgu]
        sub[s]["v"] = _layer_norm(zs[:, d_sgu:], sgug_ref[...], sgub_ref[...])

    def st_sgu(s):
        s_rows = []
        for b in range(nb):
            parts = []
            for q in range(n_hgrp):
                rhs = jnp.concatenate(
                    [jnp.where(lane_head == i,
                               sub[c]["v"][b * ts:(b + 1) * ts, q * grp_w:(q + 1) * grp_w], 0.0)
                     for i in range(SGU_HEAD_GROUP) for c in range(s + 1)], axis=0)
                parts.append(jnp.dot(wsgu_refs[s][q], rhs.astype(BF16),
                                     preferred_element_type=F32))
            s_rows.append(jnp.concatenate(parts, axis=-1) + bsgu_ref[s * ts:(s + 1) * ts, :])
        y_sgu = sub[s]["u"] * jnp.concatenate(s_rows, axis=0)
        sub[s]["n_sgu"] = _rms_norm(y_sgu, gsgu_ref[...]).astype(BF16)

    def st_out(s):
        v = sub[s]
        mix = (jnp.dot(v["n_ssm"], wout_ref[:d_ssm, :], preferred_element_type=F32)
               + jnp.dot(v["n_sgu"], wout_ref[d_ssm:, :], preferred_element_type=F32))
        out = _layer_norm(alpha * v["h"] + mix, ln1g_ref[...], ln1b_ref[...])
        out_ref[:, s * ts:(s + 1) * ts, :] = out.reshape(nb, ts, d)

    for stage_group in ((st_in,), (st_zt, st_sgu_pre, st_b), (st_scan, st_c),
                        (st_glu, st_sgu), (st_out,)):
        for s in range(n_sub):
            for stage in stage_group:
                stage(s)

    for j, lo in chains:
        state_ref[j, :, lo:lo + SCAN_W] = state[(j, lo)][0]
        state_ref[j, :, n_cplx + lo:n_cplx + lo + SCAN_W] = state[(j, lo)][1]


def _layer_spec(stacked, layer):
    zeros = (0,) * (stacked.ndim - 1)
    return pl.BlockSpec((None,) + stacked.shape[1:], lambda *_: (layer,) + zeros,
                        pipeline_mode=pl.Buffered(1))


def _whole_spec(arr):
    zeros = (0,) * arr.ndim
    return pl.BlockSpec(arr.shape, lambda *_: zeros, pipeline_mode=pl.Buffered(1))


def _mixer_call(h, w, layer, alpha, emb=None):
    nb, seq, d = h.shape
    d_ssm = w["dskip"].shape[-1]
    _, n_sb, k_sb, n_state = w["bfull"].shape
    n_sub = len(w["wsgu"])
    consts = ([w["win"], w["bfull"], w["cfull"], w["are"], w["aim"], w["dskip"], w["wglu"],
               w["sgug"], w["sgub"]] + list(w["wsgu"])
              + [w["bsgu"], w["gssm"], w["gsgu"], w["wout"], w["ln1g"], w["ln1b"]])
    emb = () if emb is None else tuple(emb)
    rows = nb * T_TILE // n_sub
    return pl.pallas_call(
        functools.partial(_mixer_kernel, alpha=alpha, n_sub=n_sub, embed=bool(emb)),
        grid=(seq // T_TILE,),
        in_specs=[_whole_spec(e) for e in emb]
                 + [pl.BlockSpec((nb, T_TILE, d), lambda c: (0, c, 0))]
                 + [_layer_spec(x, layer) for x in consts],
        out_specs=pl.BlockSpec((nb, T_TILE, d), lambda c: (0, c, 0)),
        out_shape=jax.ShapeDtypeStruct(h.shape, F32),
        scratch_shapes=[
            pltpu.VMEM((n_sub, d_ssm // LANES, rows, LANES), F32),
            pltpu.VMEM((n_sub, n_sb, rows, n_state), F32),
            pltpu.VMEM((n_sub, n_sb, rows, n_state), BF16),
            pltpu.VMEM((n_sub, d_ssm // LANES, rows, LANES), F32),
            pltpu.VMEM((n_sb, nb, n_state), F32),
        ],
        compiler_params=pltpu.CompilerParams(
            dimension_semantics=("arbitrary",), vmem_limit_bytes=VMEM_LIMIT),
        name="mixer",
    )(*emb, h, *consts)


def _ffn_kernel(h_ref, p_ref, wg_ref, wu_ref, wd_ref, wple_ref, wpg_ref, bpg_ref,
                ln2g_ref, ln2b_ref, out_ref, *, alpha):
    d_ff = wg_ref.shape[1]
    for r0 in range(0, h_ref.shape[0], FFN_SUB_ROWS):
        rs = slice(r0, r0 + FFN_SUB_ROWS)
        h = h_ref[rs, :]
        hb = h.astype(BF16)
        ple_lin = jnp.dot(p_ref[rs, :].astype(BF16), wple_ref[...], preferred_element_type=F32)
        ple_gate = jnp.dot(hb, wpg_ref[...], preferred_element_type=F32) + bpg_ref[...]
        acc = alpha * h + ple_lin * _sigmoid(ple_gate)
        for off in range(0, d_ff, FFN_CHUNK):
            sz = min(FFN_CHUNK, d_ff - off)
            g = jnp.dot(hb, wg_ref[:, off:off + sz], preferred_element_type=F32)
            u = jnp.dot(hb, wu_ref[:, off:off + sz], preferred_element_type=F32)
            a = (g * _sigmoid(g) * u).astype(BF16)
            acc = acc + jnp.dot(a, wd_ref[off:off + sz, :], preferred_element_type=F32)
        out_ref[rs, :] = _layer_norm(acc, ln2g_ref[...], ln2b_ref[...])


def _ffn_call(h2, p3, w, layer, alpha):
    m, d = h2.shape
    consts = [w["wg"], w["wu"], w["wd"], w["wple"], w["wpg"], w["bpg"], w["ln2g"], w["ln2b"]]
    return pl.pallas_call(
        functools.partial(_ffn_kernel, alpha=alpha),
        grid=(m // FFN_ROWS,),
        in_specs=[pl.BlockSpec((FFN_ROWS, d), lambda i: (i, 0)),
                  pl.BlockSpec((None, FFN_ROWS, p3.shape[-1]), lambda i: (layer, i, 0))]
                 + [_layer_spec(x, layer) for x in consts],
        out_specs=pl.BlockSpec((FFN_ROWS, d), lambda i: (i, 0)),
        out_shape=jax.ShapeDtypeStruct(h2.shape, F32),
        compiler_params=pltpu.CompilerParams(
            dimension_semantics=("parallel",), vmem_limit_bytes=VMEM_LIMIT),
        name="ffn",
    )(h2, p3, *consts)


def _ssm_layout(abr, abi, bbr, bbi, c_re, c_im, n_sb):
    depth, g, p, hh = bbr.shape
    gl = g // n_sb
    eye = jnp.eye(gl, dtype=F32)

    def b_mat(x):
        x = x.reshape(depth, n_sb, gl, p, hh)
        return jnp.einsum("djgph,gk->djghkp", x, eye).reshape(depth, n_sb, gl * hh, gl * p)

    def c_mat(x):
        x = x.reshape(depth, n_sb, gl, hh, p)
        return jnp.einsum("djghp,gk->djgpkh", x, eye).reshape(depth, n_sb, gl * p, gl * hh)

    bfull = jnp.concatenate([b_mat(bbr), b_mat(bbi)], axis=-1).astype(BF16)
    cfull = jnp.concatenate([c_mat(c_re), -c_mat(c_im)], axis=-2).astype(BF16)
    a_cols = lambda a: jnp.broadcast_to(
        a.reshape(depth, n_sb, 1, gl * p), (depth, n_sb, SUBLANES, gl * p)).astype(F32)
    return bfull, cfull, a_cols(abr), a_cols(abi)


def _sgu_layout(w_s, n_sub):
    depth, heads = w_s.shape[:2]
    grps = heads // SGU_HEAD_GROUP
    out = []
    for s in range(n_sub):
        blk = w_s[:, :, s * T_SUB:(s + 1) * T_SUB, :(s + 1) * T_SUB]
        blk = blk.reshape(depth, grps, SGU_HEAD_GROUP, T_SUB, (s + 1) * T_SUB)
        out.append(jnp.transpose(blk, (0, 1, 3, 2, 4)).reshape(
            depth, grps, T_SUB, SGU_HEAD_GROUP * (s + 1) * T_SUB).astype(BF16))
    return out


def kernel(x, p, emb_ln_g, emb_ln_b, w_in, ssm_a_re, ssm_a_im, ssm_log_dt, ssm_b_re, ssm_b_im, ssm_c_re, ssm_c_im, ssm_d, ssm_w_glu, sgu_ln_g, sgu_ln_b, sgu_w_s, sgu_b_s, out_g_ssm, out_g_sgu, w_out, ln1_g, ln1_b, w_ffn_gate, w_ffn_up, w_ffn_down, w_ple, w_ple_gate, b_ple_gate, ln2_g, ln2_b):
    bsz, seq, d = x.shape
    depth = w_in.shape[0]
    assert bsz == SUBLANES and seq % T_TILE == 0
    assert sgu_w_s.shape[-1] == T_TILE and T_TILE % T_SUB == 0
    alpha = (2 * depth) ** 0.25
    d_sgu = sgu_ln_g.shape[-1]
    head_dim = d_sgu // SGU_HEADS
    n_sb = 2
    n_sub = T_TILE // T_SUB
    rows = lambda a: a.reshape(depth, 1, -1).astype(F32)

    abr, abi, bbr, bbi = _s5_prep(ssm_a_re, ssm_a_im, ssm_log_dt, ssm_b_re, ssm_b_im)
    bfull, cfull, are, aim = _ssm_layout(abr, abi, bbr, bbi, ssm_c_re, ssm_c_im, n_sb)
    wm = dict(
        win=w_in.astype(BF16), bfull=bfull, cfull=cfull, are=are, aim=aim,
        dskip=rows(ssm_d), wglu=ssm_w_glu.astype(BF16),
        sgug=rows(sgu_ln_g), sgub=rows(sgu_ln_b), wsgu=_sgu_layout(sgu_w_s, n_sub),
        bsgu=jnp.repeat(jnp.swapaxes(sgu_b_s, 1, 2), head_dim, axis=2),
        gssm=rows(out_g_ssm), gsgu=rows(out_g_sgu), wout=w_out.astype(BF16),
        ln1g=rows(ln1_g), ln1b=rows(ln1_b))
    wf = dict(
        wg=w_ffn_gate.astype(BF16), wu=w_ffn_up.astype(BF16), wd=w_ffn_down.astype(BF16),
        wple=w_ple.astype(BF16), wpg=w_ple_gate.astype(BF16), bpg=rows(b_ple_gate),
        ln2g=rows(ln2_g), ln2b=rows(ln2_b))
    p3 = p.reshape(depth, bsz * seq, -1)
    emb = (emb_ln_g.reshape(1, d), emb_ln_b.reshape(1, d))

    h = x
    for i in range(depth):
        h = _mixer_call(h.reshape(bsz, seq, d), wm, i, alpha, emb=emb if i == 0 else None)
        h = _ffn_call(h.reshape(bsz * seq, d), p3, wf, i, alpha)
    return h.reshape(bsz, seq, d)
```

```python
import functools
import math

import jax
import jax.numpy as jnp
from jax import lax
from jax.experimental import pallas as pl
from jax.experimental.pallas import tpu as pltpu

F32 = jnp.float32
BF16 = jnp.bfloat16

LN_EPS = 1e-5
CHUNK = 64
SGU_BLOCK = 128
SGU_HEADS = 8
SGU_HEAD_GROUP = 4
T_TILE = SGU_BLOCK
T_SUB = CHUNK
SUBLANES = 8
LANES = 128
SCAN_W = 512
FFN_ROWS = 1024
FFN_SUB_ROWS = 512
FFN_CHUNK = 512
VMEM_LIMIT = 56 * 1024 * 1024


def _gelu(x):
    k = -2.0 * math.sqrt(2.0 / math.pi) * math.log2(math.e)
    return x / (1.0 + jnp.exp2(x * (k + (k * 0.044715) * (x * x))))


def _sigmoid(x):
    return 1.0 / (1.0 + jnp.exp2(x * -math.log2(math.e)))


def _layer_norm(x, g, b):
    mu = jnp.mean(x, axis=-1, keepdims=True)
    xc = x - mu
    var = jnp.mean(xc * xc, axis=-1, keepdims=True)
    return xc * lax.rsqrt(var + LN_EPS) * g + b


def _rms_norm(x, g):
    return x * lax.rsqrt(jnp.mean(x * x, axis=-1, keepdims=True) + LN_EPS) * g


def _s5_prep_kernel(are_ref, aim_ref, ldt_ref, bre_ref, bim_ref,
                    abr_ref, abi_ref, bbr_ref, bbi_ref):
    a_re = are_ref[...]
    a_im = aim_ref[...]
    dt = jnp.exp(ldt_ref[...])
    mag = jnp.exp(a_re * dt)
    ab_re = mag * jnp.cos(a_im * dt)
    ab_im = mag * jnp.sin(a_im * dt)
    x_re = ab_re - 1.0
    den = a_re * a_re + a_im * a_im
    q_re = (x_re * a_re + ab_im * a_im) / den
    q_im = (ab_im * a_re - x_re * a_im) / den
    b_re = bre_ref[...]
    b_im = bim_ref[...]
    abr_ref[...] = ab_re
    abi_ref[...] = ab_im
    bbr_ref[...] = q_re * b_re - q_im * b_im
    bbi_ref[...] = q_re * b_im + q_im * b_re


def _s5_prep(a_re, a_im, log_dt, b_re, b_im):
    depth, g, p, h = b_re.shape
    shp = (depth * g, p * h)
    bc = lambda x: jnp.broadcast_to(x, (depth, g, p, h)).reshape(shp)
    ins = (bc(a_re[..., None]), bc(a_im[..., None]), bc(log_dt[..., None, None]),
           b_re.reshape(shp), b_im.reshape(shp))
    outs = pl.pallas_call(
        _s5_prep_kernel,
        out_shape=[jax.ShapeDtypeStruct(shp, F32)] * 4,
        name="s5_prep",
    )(*ins)
    abr, abi, bbr, bbi = (o.reshape(depth, g, p, h) for o in outs)
    return abr[..., 0], abi[..., 0], bbr, bbi


def _mixer_kernel(*refs, alpha, n_sub, embed):
    if embed:
        embg_ref, embb_ref = refs[:2]
        refs = refs[2:]
    (h_ref, win_ref, bfull_ref, cfull_ref, are_ref, aim_ref, dskip_ref, wglu_ref,
     sgug_ref, sgub_ref) = refs[:10]
    wsgu_refs = refs[10:10 + n_sub]
    (bsgu_ref, gssm_ref, gsgu_ref, wout_ref, out_ref,
     zt_ref, bu_ref, sb_ref, yt_ref, state_ref) = refs[10 + n_sub:]

    nb, tt, d = h_ref.shape
    ts = tt // n_sub
    rows = nb * ts
    d_ssm = dskip_ref.shape[-1]
    d_sgu = sgug_ref.shape[-1]
    n_sb = bfull_ref.shape[0]
    k_sb = bfull_ref.shape[1]
    n_cplx = bfull_ref.shape[2] // 2
    n_slab = d_ssm // LANES
    slabs_per_sb = k_sb // LANES
    head_dim = d_sgu // SGU_HEADS
    n_hgrp = wsgu_refs[0].shape[0]
    grp_w = SGU_HEAD_GROUP * head_dim
    lane_head = lax.broadcasted_iota(jnp.int32, (ts, grp_w), 1) // head_dim

    @pl.when(pl.program_id(0) == 0)
    def _():
        state_ref[...] = jnp.zeros_like(state_ref)

    chains = [(j, lo) for j in range(n_sb) for lo in range(0, n_cplx, SCAN_W)]
    state = {(j, lo): (state_ref[j, :, lo:lo + SCAN_W],
                       state_ref[j, :, n_cplx + lo:n_cplx + lo + SCAN_W]) for j, lo in chains}
    sub = [dict() for _ in range(n_sub)]

    def st_in(s):
        v = sub[s]
        v["h"] = h_ref[:, s * ts:(s + 1) * ts, :].reshape(rows, d)
        if embed:
            v["h"] = _layer_norm(v["h"], embg_ref[...], embb_ref[...])
        v["z"] = jnp.dot(v["h"].astype(BF16), win_ref[...], preferred_element_type=F32)

    def st_zt(s):
        z_ssm = sub[s]["z"][:, :d_ssm]
        for b in range(nb):
            for k in range(n_slab):
                zt_ref[s, k, pl.ds(b, ts, stride=nb), :] = (
                    z_ssm[b * ts:(b + 1) * ts, k * LANES:(k + 1) * LANES])

    def st_b(s):
        for j in range(n_sb):
            zj = jnp.concatenate(
                [zt_ref[s, j * slabs_per_sb + k] for k in range(slabs_per_sb)], axis=-1)
            bu_ref[s, j] = jnp.dot(zj.astype(BF16), bfull_ref[j], preferred_element_type=F32)

    def st_scan(s):
        for j, lo in chains:
            re_sl = slice(lo, lo + SCAN_W)
            im_sl = slice(n_cplx + lo, n_cplx + lo + SCAN_W)
            a_re = are_ref[j, :, re_sl]
            a_im = aim_ref[j, :, re_sl]
            s_re, s_im = state[(j, lo)]
            for t in range(0, ts, 2):
                r0 = slice(t * nb, (t + 1) * nb)
                r1 = slice((t + 1) * nb, (t + 2) * nb)
                m_re = a_re * s_re - a_im * s_im + bu_ref[s, j, r0, re_sl]
                m_im = a_re * s_im + a_im * s_re + bu_ref[s, j, r0, im_sl]
                s_re = a_re * m_re - a_im * m_im + bu_ref[s, j, r1, re_sl]
                s_im = a_re * m_im + a_im * m_re + bu_ref[s, j, r1, im_sl]
                r01 = slice(t * nb, (t + 2) * nb)
                sb_ref[s, j, r01, re_sl] = jnp.concatenate([m_re, s_re], axis=0).astype(BF16)
                sb_ref[s, j, r01, im_sl] = jnp.concatenate([m_im, s_im], axis=0).astype(BF16)
            state[(j, lo)] = (s_re, s_im)

    def st_c(s):
        for j in range(n_sb):
            yj = jnp.dot(sb_ref[s, j], cfull_ref[j], preferred_element_type=F32)
            for k in range(slabs_per_sb):
                yt_ref[s, j * slabs_per_sb + k] = yj[:, k * LANES:(k + 1) * LANES]
        y_lin = jnp.concatenate(
            [jnp.concatenate([yt_ref[s, k, pl.ds(b, ts, stride=nb), :] for k in range(n_slab)],
                             axis=-1) for b in range(nb)], axis=0)
        sub[s]["y"] = _gelu(y_lin + dskip_ref[...] * sub[s]["z"][:, :d_ssm])

    def st_glu(s):
        y = sub[s]["y"]
        gate = jnp.dot(y.astype(BF16), wglu_ref[...], preferred_element_type=F32)
        sub[s]["n_ssm"] = _rms_norm(y * _sigmoid(gate), gssm_ref[...]).astype(BF16)

    def st_sgu_pre(s):
        zs = _gelu(sub[s]["z"][:, d_ssm:])
        sub[s]["u"] = zs[:, :d_sgu]
        sub[s]["v"] = _layer_norm(zs[:, d_sgu:], sgug_ref[...], sgub_ref[...])

    def st_sgu(s):
        s_rows = []
        for b in range(nb):
            parts = []
            for q in range(n_hgrp):
                rhs = jnp.concatenate(
                    [jnp.where(lane_head == i,
                               sub[c]["v"][b * ts:(b + 1) * ts, q * grp_w:(q + 1) * grp_w], 0.0)
                     for i in range(SGU_HEAD_GROUP) for c in range(s + 1)], axis=0)
                parts.append(jnp.dot(wsgu_refs[s][q], rhs.astype(BF16),
                                     preferred_element_type=F32))
            s_rows.append(jnp.concatenate(parts, axis=-1) + bsgu_ref[s * ts:(s + 1) * ts, :])
        y_sgu = sub[s]["u"] * jnp.concatenate(s_rows, axis=0)
        sub[s]["n_sgu"] = _rms_norm(y_sgu, gsgu_ref[...]).astype(BF16)

    def st_out(s):
        v = sub[s]
        mix = (jnp.dot(v["n_ssm"], wout_ref[:d_ssm, :], preferred_element_type=F32)
               + jnp.dot(v["n_sgu"], wout_ref[d_ssm:, :], preferred_element_type=F32))
        out_ref[:, s * ts:(s + 1) * ts, :] = (alpha * v["h"] + mix).reshape(nb, ts, d)

    for stage_group in ((st_in,), (st_zt, st_sgu_pre, st_b), (st_scan, st_c),
                        (st_glu, st_sgu), (st_out,)):
        for s in range(n_sub):
            for stage in stage_group:
                stage(s)

    for j, lo in chains:
        state_ref[j, :, lo:lo + SCAN_W] = state[(j, lo)][0]
        state_ref[j, :, n_cplx + lo:n_cplx + lo + SCAN_W] = state[(j, lo)][1]


def _layer_spec(stacked, layer):
    zeros = (0,) * (stacked.ndim - 1)
    return pl.BlockSpec((None,) + stacked.shape[1:], lambda *_: (layer,) + zeros,
                        pipeline_mode=pl.Buffered(1))


def _whole_spec(arr):
    zeros = (0,) * arr.ndim
    return pl.BlockSpec(arr.shape, lambda *_: zeros, pipeline_mode=pl.Buffered(1))


def _mixer_call(h, w, layer, alpha, emb=None):
    nb, seq, d = h.shape
    d_ssm = w["dskip"].shape[-1]
    _, n_sb, k_sb, n_state = w["bfull"].shape
    n_sub = len(w["wsgu"])
    consts = ([w["win"], w["bfull"], w["cfull"], w["are"], w["aim"], w["dskip"], w["wglu"],
               w["sgug"], w["sgub"]] + list(w["wsgu"])
              + [w["bsgu"], w["gssm"], w["gsgu"], w["wout"]])
    emb = () if emb is None else tuple(emb)
    rows = nb * T_TILE // n_sub
    return pl.pallas_call(
        functools.partial(_mixer_kernel, alpha=alpha, n_sub=n_sub, embed=bool(emb)),
        grid=(seq // T_TILE,),
        in_specs=[_whole_spec(e) for e in emb]
                 + [pl.BlockSpec((nb, T_TILE, d), lambda c: (0, c, 0))]
                 + [_layer_spec(x, layer) for x in consts],
        out_specs=pl.BlockSpec((nb, T_TILE, d), lambda c: (0, c, 0)),
        out_shape=jax.ShapeDtypeStruct(h.shape, F32),
        scratch_shapes=[
            pltpu.VMEM((n_sub, d_ssm // LANES, rows, LANES), F32),
            pltpu.VMEM((n_sub, n_sb, rows, n_state), F32),
            pltpu.VMEM((n_sub, n_sb, rows, n_state), BF16),
            pltpu.VMEM((n_sub, d_ssm // LANES, rows, LANES), F32),
            pltpu.VMEM((n_sb, nb, n_state), F32),
        ],
        compiler_params=pltpu.CompilerParams(
            dimension_semantics=("arbitrary",), vmem_limit_bytes=VMEM_LIMIT),
        name="mixer",
    )(*emb, h, *consts)


def _ffn_kernel(x_ref, p_ref, ln1g_ref, ln1b_ref, wg_ref, wu_ref, wd_ref, wple_ref, wpg_ref,
                bpg_ref, ln2g_ref, ln2b_ref, out_ref, *, alpha):
    d_ff = wg_ref.shape[1]
    for r0 in range(0, x_ref.shape[0], FFN_SUB_ROWS):
        rs = slice(r0, r0 + FFN_SUB_ROWS)
        h = _layer_norm(x_ref[rs, :], ln1g_ref[...], ln1b_ref[...])
        hb = h.astype(BF16)
        ple_lin = jnp.dot(p_ref[rs, :].astype(BF16), wple_ref[...], preferred_element_type=F32)
        ple_gate = jnp.dot(hb, wpg_ref[...], preferred_element_type=F32) + bpg_ref[...]
        acc = alpha * h + ple_lin * _sigmoid(ple_gate)
        for off in range(0, d_ff, FFN_CHUNK):
            sz = min(FFN_CHUNK, d_ff - off)
            g = jnp.dot(hb, wg_ref[:, off:off + sz], preferred_element_type=F32)
            u = jnp.dot(hb, wu_ref[:, off:off + sz], preferred_element_type=F32)
            a = (g * _sigmoid(g) * u).astype(BF16)
            acc = acc + jnp.dot(a, wd_ref[off:off + sz, :], preferred_element_type=F32)
        out_ref[rs, :] = _layer_norm(acc, ln2g_ref[...], ln2b_ref[...])


def _ffn_call(h2, p3, w, layer, alpha):
    m, d = h2.shape
    consts = [w["ln1g"], w["ln1b"], w["wg"], w["wu"], w["wd"], w["wple"], w["wpg"], w["bpg"],
              w["ln2g"], w["ln2b"]]
    return pl.pallas_call(
        functools.partial(_ffn_kernel, alpha=alpha),
        grid=(m // FFN_ROWS,),
        in_specs=[pl.BlockSpec((FFN_ROWS, d), lambda i: (i, 0)),
                  pl.BlockSpec((None, FFN_ROWS, p3.shape[-1]), lambda i: (layer, i, 0))]
                 + [_layer_spec(x, layer) for x in consts],
        out_specs=pl.BlockSpec((FFN_ROWS, d), lambda i: (i, 0)),
        out_shape=jax.ShapeDtypeStruct(h2.shape, F32),
        compiler_params=pltpu.CompilerParams(
            dimension_semantics=("parallel",), vmem_limit_bytes=VMEM_LIMIT),
        name="ffn",
    )(h2, p3, *consts)


def _ssm_layout(abr, abi, bbr, bbi, c_re, c_im, n_sb):
    depth, g, p, hh = bbr.shape
    gl = g // n_sb
    eye = jnp.eye(gl, dtype=F32)

    def b_mat(x):
        x = x.reshape(depth, n_sb, gl, p, hh)
        return jnp.einsum("djgph,gk->djghkp", x, eye).reshape(depth, n_sb, gl * hh, gl * p)

    def c_mat(x):
        x = x.reshape(depth, n_sb, gl, hh, p)
        return jnp.einsum("djghp,gk->djgpkh", x, eye).reshape(depth, n_sb, gl * p, gl * hh)

    bfull = jnp.concatenate([b_mat(bbr), b_mat(bbi)], axis=-1).astype(BF16)
    cfull = jnp.concatenate([c_mat(c_re), -c_mat(c_im)], axis=-2).astype(BF16)
    a_cols = lambda a: jnp.broadcast_to(
        a.reshape(depth, n_sb, 1, gl * p), (depth, n_sb, SUBLANES, gl * p)).astype(F32)
    return bfull, cfull, a_cols(abr), a_cols(abi)


def _sgu_layout(w_s, n_sub):
    depth, heads = w_s.shape[:2]
    grps = heads // SGU_HEAD_GROUP
    out = []
    for s in range(n_sub):
        blk = w_s[:, :, s * T_SUB:(s + 1) * T_SUB, :(s + 1) * T_SUB]
        blk = blk.reshape(depth, grps, SGU_HEAD_GROUP, T_SUB, (s + 1) * T_SUB)
        out.append(jnp.transpose(blk, (0, 1, 3, 2, 4)).reshape(
            depth, grps, T_SUB, SGU_HEAD_GROUP * (s + 1) * T_SUB).astype(BF16))
    return out


def kernel(x, p, emb_ln_g, emb_ln_b, w_in, ssm_a_re, ssm_a_im, ssm_log_dt, ssm_b_re, ssm_b_im, ssm_c_re, ssm_c_im, ssm_d, ssm_w_glu, sgu_ln_g, sgu_ln_b, sgu_w_s, sgu_b_s, out_g_ssm, out_g_sgu, w_out, ln1_g, ln1_b, w_ffn_gate, w_ffn_up, w_ffn_down, w_ple, w_ple_gate, b_ple_gate, ln2_g, ln2_b):
    bsz, seq, d = x.shape
    depth = w_in.shape[0]
    assert bsz == SUBLANES and seq % T_TILE == 0
    assert sgu_w_s.shape[-1] == T_TILE and T_TILE % T_SUB == 0
    alpha = (2 * depth) ** 0.25
    d_sgu = sgu_ln_g.shape[-1]
    head_dim = d_sgu // SGU_HEADS
    n_sb = 2
    n_sub = T_TILE // T_SUB
    rows = lambda a: a.reshape(depth, 1, -1).astype(F32)

    abr, abi, bbr, bbi = _s5_prep(ssm_a_re, ssm_a_im, ssm_log_dt, ssm_b_re, ssm_b_im)
    bfull, cfull, are, aim = _ssm_layout(abr, abi, bbr, bbi, ssm_c_re, ssm_c_im, n_sb)
    wm = dict(
        win=w_in.astype(BF16), bfull=bfull, cfull=cfull, are=are, aim=aim,
        dskip=rows(ssm_d), wglu=ssm_w_glu.astype(BF16),
        sgug=rows(sgu_ln_g), sgub=rows(sgu_ln_b), wsgu=_sgu_layout(sgu_w_s, n_sub),
        bsgu=jnp.repeat(jnp.swapaxes(sgu_b_s, 1, 2), head_dim, axis=2),
        gssm=rows(out_g_ssm), gsgu=rows(out_g_sgu), wout=w_out.astype(BF16))
    wf = dict(
        ln1g=rows(ln1_g), ln1b=rows(ln1_b),
        wg=w_ffn_gate.astype(BF16), wu=w_ffn_up.astype(BF16), wd=w_ffn_down.astype(BF16),
        wple=w_ple.astype(BF16), wpg=w_ple_gate.astype(BF16), bpg=rows(b_ple_gate),
        ln2g=rows(ln2_g), ln2b=rows(ln2_b))
    p3 = p.reshape(depth, bsz * seq, -1)
    emb = (emb_ln_g.reshape(1, d), emb_ln_b.reshape(1, d))

    h = x
    for i in range(depth):
        h = _mixer_call(h.reshape(bsz, seq, d), wm, i, alpha, emb=emb if i == 0 else None)
        h = _ffn_call(h.reshape(bsz * seq, d), p3, wf, i, alpha)
    return h.reshape(bsz, seq, d)
```

```python
import functools
import math

import jax
import jax.numpy as jnp
from jax import lax
from jax.experimental import pallas as pl
from jax.experimental.pallas import tpu as pltpu

F32 = jnp.float32
BF16 = jnp.bfloat16

LN_EPS = 1e-5
CHUNK = 64
SGU_BLOCK = 128
SGU_HEADS = 8
SGU_HEAD_GROUP = 4
T_TILE = SGU_BLOCK
T_SUB = CHUNK
SUBLANES = 8
LANES = 128
SCAN_W = 512
FFN_ROWS = 1024
FFN_SUB_ROWS = 512
FFN_CHUNK = 512
FFN_LAG = 2
VMEM_LIMIT = 56 * 1024 * 1024


def _gelu(x):
    k = -2.0 * math.sqrt(2.0 / math.pi) * math.log2(math.e)
    return x / (1.0 + jnp.exp2(x * (k + (k * 0.044715) * (x * x))))


def _sigmoid(x):
    return 1.0 / (1.0 + jnp.exp2(x * -math.log2(math.e)))


def _layer_norm(x, g, b):
    mu = jnp.mean(x, axis=-1, keepdims=True)
    xc = x - mu
    var = jnp.mean(xc * xc, axis=-1, keepdims=True)
    return xc * lax.rsqrt(var + LN_EPS) * g + b


def _rms_norm(x, g):
    return x * lax.rsqrt(jnp.mean(x * x, axis=-1, keepdims=True) + LN_EPS) * g


def _s5_prep_kernel(are_ref, aim_ref, ldt_ref, bre_ref, bim_ref,
                    abr_ref, abi_ref, bbr_ref, bbi_ref):
    a_re = are_ref[...]
    a_im = aim_ref[...]
    dt = jnp.exp(ldt_ref[...])
    mag = jnp.exp(a_re * dt)
    ab_re = mag * jnp.cos(a_im * dt)
    ab_im = mag * jnp.sin(a_im * dt)
    x_re = ab_re - 1.0
    den = a_re * a_re + a_im * a_im
    q_re = (x_re * a_re + ab_im * a_im) / den
    q_im = (ab_im * a_re - x_re * a_im) / den
    b_re = bre_ref[...]
    b_im = bim_ref[...]
    abr_ref[...] = ab_re
    abi_ref[...] = ab_im
    bbr_ref[...] = q_re * b_re - q_im * b_im
    bbi_ref[...] = q_re * b_im + q_im * b_re


def _s5_prep(a_re, a_im, log_dt, b_re, b_im):
    depth, g, p, h = b_re.shape
    shp = (depth * g, h * p)
    bc = lambda x: jnp.broadcast_to(x, (depth, g, h, p)).reshape(shp)
    tr = lambda x: jnp.swapaxes(x, -1, -2).reshape(shp)
    ins = (bc(a_re[:, :, None, :]), bc(a_im[:, :, None, :]), bc(log_dt[..., None, None]),
           tr(b_re), tr(b_im))
    outs = pl.pallas_call(
        _s5_prep_kernel,
        out_shape=[jax.ShapeDtypeStruct(shp, F32)] * 4,
        name="s5_prep",
    )(*ins)
    abr, abi, bbr, bbi = (o.reshape(depth, g, h, p) for o in outs)
    return abr[:, :, 0, :], abi[:, :, 0, :], bbr, bbi


def _mixer_kernel(*refs, alpha, n_sub, embed):
    if embed:
        embg_ref, embb_ref = refs[:2]
        refs = refs[2:]
    (h_ref, win_ref, bfull_ref, cfull_ref, are_ref, aim_ref, dskip_ref, wglu_ref,
     sgug_ref, sgub_ref) = refs[:10]
    wsgu_refs = refs[10:10 + n_sub]
    (bsgu_ref, gssm_ref, gsgu_ref, wout_ref, out_ref,
     zt_ref, bu_ref, sb_ref, yt_ref, state_ref) = refs[10 + n_sub:]

    nb, tt, d = h_ref.shape
    ts = tt // n_sub
    rows = nb * ts
    d_ssm = dskip_ref.shape[-1]
    d_sgu = sgug_ref.shape[-1]
    n_sb = bfull_ref.shape[0]
    k_sb = bfull_ref.shape[1]
    n_cplx = bfull_ref.shape[2] // 2
    n_slab = d_ssm // LANES
    slabs_per_sb = k_sb // LANES
    head_dim = d_sgu // SGU_HEADS
    n_hgrp = wsgu_refs[0].shape[0]
    grp_w = SGU_HEAD_GROUP * head_dim
    lane_head = lax.broadcasted_iota(jnp.int32, (ts, grp_w), 1) // head_dim

    @pl.when(pl.program_id(0) == 0)
    def _():
        state_ref[...] = jnp.zeros_like(state_ref)

    chains = [(j, lo) for j in range(n_sb) for lo in range(0, n_cplx, SCAN_W)]
    state = {(j, lo): (state_ref[j, :, lo:lo + SCAN_W],
                       state_ref[j, :, n_cplx + lo:n_cplx + lo + SCAN_W]) for j, lo in chains}
    sub = [dict() for _ in range(n_sub)]

    def st_in(s):
        v = sub[s]
        v["h"] = h_ref[:, s * ts:(s + 1) * ts, :].reshape(rows, d)
        if embed:
            v["h"] = _layer_norm(v["h"], embg_ref[...], embb_ref[...])
        v["z"] = jnp.dot(v["h"].astype(BF16), win_ref[...], preferred_element_type=F32)

    def st_zt(s):
        z_ssm = sub[s]["z"][:, :d_ssm]
        for b in range(nb):
            for k in range(n_slab):
                zt_ref[s, k, pl.ds(b, ts, stride=nb), :] = (
                    z_ssm[b * ts:(b + 1) * ts, k * LANES:(k + 1) * LANES])

    def st_b(s):
        for j in range(n_sb):
            zj = jnp.concatenate(
                [zt_ref[s, j * slabs_per_sb + k] for k in range(slabs_per_sb)], axis=-1)
            bu_ref[s, j] = jnp.dot(zj.astype(BF16), bfull_ref[j], preferred_element_type=F32)

    def st_scan(s):
        for j, lo in chains:
            re_sl = slice(lo, lo + SCAN_W)
            im_sl = slice(n_cplx + lo, n_cplx + lo + SCAN_W)
            a_re = are_ref[j, :, re_sl]
            a_im = aim_ref[j, :, re_sl]
            s_re, s_im = state[(j, lo)]
            for t in range(0, ts, 2):
                r0 = slice(t * nb, (t + 1) * nb)
                r1 = slice((t + 1) * nb, (t + 2) * nb)
                m_re = a_re * s_re - a_im * s_im + bu_ref[s, j, r0, re_sl]
                m_im = a_re * s_im + a_im * s_re + bu_ref[s, j, r0, im_sl]
                s_re = a_re * m_re - a_im * m_im + bu_ref[s, j, r1, re_sl]
                s_im = a_re * m_im + a_im * m_re + bu_ref[s, j, r1, im_sl]
                r01 = slice(t * nb, (t + 2) * nb)
                sb_ref[s, j, r01, re_sl] = jnp.concatenate([m_re, s_re], axis=0).astype(BF16)
                sb_ref[s, j, r01, im_sl] = jnp.concatenate([m_im, s_im], axis=0).astype(BF16)
            state[(j, lo)] = (s_re, s_im)

    def st_c(s):
        for j in range(n_sb):
            yj = jnp.dot(sb_ref[s, j], cfull_ref[j], preferred_element_type=F32)
            for k in range(slabs_per_sb):
                yt_ref[s, j * slabs_per_sb + k] = yj[:, k * LANES:(k + 1) * LANES]
        y_lin = jnp.concatenate(
            [jnp.concatenate([yt_ref[s, k, pl.ds(b, ts, stride=nb), :] for k in range(n_slab)],
                             axis=-1) for b in range(nb)], axis=0)
        sub[s]["y"] = _gelu(y_lin + dskip_ref[...] * sub[s]["z"][:, :d_ssm])

    def st_glu(s):
        y = sub[s]["y"]
        gate = jnp.dot(y.astype(BF16), wglu_ref[...], preferred_element_type=F32)
        sub[s]["n_ssm"] = _rms_norm(y * _sigmoid(gate), gssm_ref[...]).astype(BF16)

    def st_sgu_pre(s):
        zs = _gelu(sub[s]["z"][:, d_ssm:])
        sub[s]["u"] = zs[:, :d_sgu]
        sub[s]["v"] = _layer_norm(zs[:, d_sgu:], sgug_ref[...], sgub_ref[...])

    def st_sgu(s):
        s_rows = []
        for b in range(nb):
            parts = []
            for q in range(n_hgrp):
                rhs = jnp.concatenate(
                    [jnp.where(lane_head == i,
                               sub[c]["v"][b * ts:(b + 1) * ts, q * grp_w:(q + 1) * grp_w], 0.0)
                     for i in range(SGU_HEAD_GROUP) for c in range(s + 1)], axis=0)
                parts.append(jnp.dot(wsgu_refs[s][q], rhs.astype(BF16),
                                     preferred_element_type=F32))
            s_rows.append(jnp.concatenate(parts, axis=-1) + bsgu_ref[s * ts:(s + 1) * ts, :])
        y_sgu = sub[s]["u"] * jnp.concatenate(s_rows, axis=0)
        sub[s]["n_sgu"] = _rms_norm(y_sgu, gsgu_ref[...]).astype(BF16)

    def st_out(s):
        v = sub[s]
        mix = (jnp.dot(v["n_ssm"], wout_ref[:d_ssm, :], preferred_element_type=F32)
               + jnp.dot(v["n_sgu"], wout_ref[d_ssm:, :], preferred_element_type=F32))
        out_ref[:, s * ts:(s + 1) * ts, :] = (alpha * v["h"] + mix).reshape(nb, ts, d)

    for stage_group in ((st_in,), (st_zt, st_sgu_pre, st_b), (st_scan, st_c),
                        (st_glu, st_sgu), (st_out,)):
        for s in range(n_sub):
            for stage in stage_group:
                stage(s)

    for j, lo in chains:
        state_ref[j, :, lo:lo + SCAN_W] = state[(j, lo)][0]
        state_ref[j, :, n_cplx + lo:n_cplx + lo + SCAN_W] = state[(j, lo)][1]


def _layer_spec(stacked, layer):
    zeros = (0,) * (stacked.ndim - 1)
    return pl.BlockSpec((None,) + stacked.shape[1:], lambda *_: (layer,) + zeros,
                        pipeline_mode=pl.Buffered(1))


def _whole_spec(arr):
    zeros = (0,) * arr.ndim
    return pl.BlockSpec(arr.shape, lambda *_: zeros, pipeline_mode=pl.Buffered(1))


def _mixer_call(h, w, layer, alpha, emb=None):
    nb, seq, d = h.shape
    d_ssm = w["dskip"].shape[-1]
    _, n_sb, k_sb, n_state = w["bfull"].shape
    n_sub = len(w["wsgu"])
    consts = ([w["win"], w["bfull"], w["cfull"], w["are"], w["aim"], w["dskip"], w["wglu"],
               w["sgug"], w["sgub"]] + list(w["wsgu"])
              + [w["bsgu"], w["gssm"], w["gsgu"], w["wout"]])
    emb = () if emb is None else tuple(emb)
    rows = nb * T_TILE // n_sub
    return pl.pallas_call(
        functools.partial(_mixer_kernel, alpha=alpha, n_sub=n_sub, embed=bool(emb)),
        grid=(seq // T_TILE,),
        in_specs=[_whole_spec(e) for e in emb]
                 + [pl.BlockSpec((nb, T_TILE, d), lambda c: (0, c, 0))]
                 + [_layer_spec(x, layer) for x in consts],
        out_specs=pl.BlockSpec((nb, T_TILE, d), lambda c: (0, c, 0)),
        out_shape=jax.ShapeDtypeStruct(h.shape, F32),
        scratch_shapes=[
            pltpu.VMEM((n_sub, d_ssm // LANES, rows, LANES), F32),
            pltpu.VMEM((n_sub, n_sb, rows, n_state), F32),
            pltpu.VMEM((n_sub, n_sb, rows, n_state), BF16),
            pltpu.VMEM((n_sub, d_ssm // LANES, rows, LANES), F32),
            pltpu.VMEM((n_sb, nb, n_state), F32),
        ],
        compiler_params=pltpu.CompilerParams(
            dimension_semantics=("arbitrary",), vmem_limit_bytes=VMEM_LIMIT),
        name="mixer",
    )(*emb, h, *consts)


def _ffn_kernel(x_ref, p_ref, ln1g_ref, ln1b_ref, wg_ref, wu_ref, wd_ref, wple_ref, wpg_ref,
                bpg_ref, ln2g_ref, ln2b_ref, out_ref, *, alpha):
    d_ff = wg_ref.shape[1]
    blocks = [slice(r0, r0 + FFN_SUB_ROWS) for r0 in range(0, x_ref.shape[0], FFN_SUB_ROWS)]
    chunks = [(off, min(FFN_CHUNK, d_ff - off)) for off in range(0, d_ff, FFN_CHUNK)]
    hb, acc = {}, {}

    def head(blk):
        rs = blocks[blk]
        h = _layer_norm(x_ref[rs, :], ln1g_ref[...], ln1b_ref[...])
        hb[blk] = h.astype(BF16)
        ple_lin = jnp.dot(p_ref[rs, :].astype(BF16), wple_ref[...], preferred_element_type=F32)
        ple_gate = jnp.dot(hb[blk], wpg_ref[...], preferred_element_type=F32) + bpg_ref[...]
        acc[blk] = alpha * h + ple_lin * _sigmoid(ple_gate)

    def chunk(blk, c):
        off, sz = chunks[c]
        g = jnp.dot(hb[blk], wg_ref[:, off:off + sz], preferred_element_type=F32)
        u = jnp.dot(hb[blk], wu_ref[:, off:off + sz], preferred_element_type=F32)
        a = (g * _sigmoid(g) * u).astype(BF16)
        acc[blk] = acc[blk] + jnp.dot(a, wd_ref[off:off + sz, :], preferred_element_type=F32)

    def tail(blk):
        out_ref[blocks[blk], :] = _layer_norm(acc[blk], ln2g_ref[...], ln2b_ref[...])

    n_c = len(chunks)
    for step in range(n_c + FFN_LAG * (len(blocks) - 1) + 1):
        for blk in range(len(blocks)):
            c = step - FFN_LAG * blk
            if c == 0:
                head(blk)
            if 0 <= c < n_c:
                chunk(blk, c)
            if c == n_c:
                tail(blk)


def _ffn_call(h2, p3, w, layer, alpha):
    m, d = h2.shape
    consts = [w["ln1g"], w["ln1b"], w["wg"], w["wu"], w["wd"], w["wple"], w["wpg"], w["bpg"],
              w["ln2g"], w["ln2b"]]
    return pl.pallas_call(
        functools.partial(_ffn_kernel, alpha=alpha),
        grid=(m // FFN_ROWS,),
        in_specs=[pl.BlockSpec((FFN_ROWS, d), lambda i: (i, 0)),
                  pl.BlockSpec((None, FFN_ROWS, p3.shape[-1]), lambda i: (layer, i, 0))]
                 + [_layer_spec(x, layer) for x in consts],
        out_specs=pl.BlockSpec((FFN_ROWS, d), lambda i: (i, 0)),
        out_shape=jax.ShapeDtypeStruct(h2.shape, F32),
        compiler_params=pltpu.CompilerParams(
            dimension_semantics=("parallel",), vmem_limit_bytes=VMEM_LIMIT),
        name="ffn",
    )(h2, p3, *consts)


def _ssm_layout(abr, abi, bbr, bbi, c_re, c_im, n_sb):
    depth, g, hh, p = bbr.shape
    gl = g // n_sb

    def block_diag(x, r, c):
        x = jnp.tile(x.reshape(depth, n_sb, gl * r, c), (1, 1, 1, gl))
        row_g = lax.broadcasted_iota(jnp.int32, (gl * r, gl * c), 0) // r
        col_g = lax.broadcasted_iota(jnp.int32, (gl * r, gl * c), 1) // c
        return jnp.where(row_g == col_g, x, 0.0)

    b_mat = lambda x: block_diag(x, hh, p)
    c_mat = lambda x: block_diag(jnp.swapaxes(x, -1, -2), p, hh)
    bfull = jnp.concatenate([b_mat(bbr), b_mat(bbi)], axis=-1).astype(BF16)
    cfull = jnp.concatenate([c_mat(c_re), -c_mat(c_im)], axis=-2).astype(BF16)
    a_cols = lambda a: jnp.broadcast_to(
        a.reshape(depth, n_sb, 1, gl * p), (depth, n_sb, SUBLANES, gl * p)).astype(F32)
    return bfull, cfull, a_cols(abr), a_cols(abi)


def _sgu_layout(w_s, n_sub):
    depth, heads = w_s.shape[:2]
    grps = heads // SGU_HEAD_GROUP
    out = []
    for s in range(n_sub):
        blk = w_s[:, :, s * T_SUB:(s + 1) * T_SUB, :(s + 1) * T_SUB]
        blk = blk.reshape(depth, grps, SGU_HEAD_GROUP, T_SUB, (s + 1) * T_SUB)
        out.append(jnp.transpose(blk, (0, 1, 3, 2, 4)).reshape(
            depth, grps, T_SUB, SGU_HEAD_GROUP * (s + 1) * T_SUB).astype(BF16))
    return out


def kernel(x, p, emb_ln_g, emb_ln_b, w_in, ssm_a_re, ssm_a_im, ssm_log_dt, ssm_b_re, ssm_b_im, ssm_c_re, ssm_c_im, ssm_d, ssm_w_glu, sgu_ln_g, sgu_ln_b, sgu_w_s, sgu_b_s, out_g_ssm, out_g_sgu, w_out, ln1_g, ln1_b, w_ffn_gate, w_ffn_up, w_ffn_down, w_ple, w_ple_gate, b_ple_gate, ln2_g, ln2_b):
    bsz, seq, d = x.shape
    depth = w_in.shape[0]
    assert bsz == SUBLANES and seq % T_TILE == 0
    assert sgu_w_s.shape[-1] == T_TILE and T_TILE % T_SUB == 0
    alpha = (2 * depth) ** 0.25
    d_sgu = sgu_ln_g.shape[-1]
    head_dim = d_sgu // SGU_HEADS
    n_sb = 2
    rows = lambda a: a.reshape(depth, 1, -1).astype(F32)

    abr, abi, bbr, bbi = _s5_prep(ssm_a_re, ssm_a_im, ssm_log_dt, ssm_b_re, ssm_b_im)
    bfull, cfull, are, aim = _ssm_layout(abr, abi, bbr, bbi, ssm_c_re, ssm_c_im, n_sb)
    wm = dict(
        win=w_in.astype(BF16), bfull=bfull, cfull=cfull, are=are, aim=aim,
        dskip=rows(ssm_d), wglu=ssm_w_glu.astype(BF16),
        sgug=rows(sgu_ln_g), sgub=rows(sgu_ln_b),
        wsgu=_sgu_layout(sgu_w_s, T_TILE // T_SUB),
        bsgu=jnp.repeat(jnp.swapaxes(sgu_b_s, 1, 2), head_dim, axis=2),
        gssm=rows(out_g_ssm), gsgu=rows(out_g_sgu), wout=w_out.astype(BF16))
    wf = dict(
        ln1g=rows(ln1_g), ln1b=rows(ln1_b),
        wg=w_ffn_gate.astype(BF16), wu=w_ffn_up.astype(BF16), wd=w_ffn_down.astype(BF16),
        wple=w_ple.astype(BF16), wpg=w_ple_gate.astype(BF16), bpg=rows(b_ple_gate),
        ln2g=rows(ln2_g), ln2b=rows(ln2_b))
    p3 = p.reshape(depth, bsz * seq, -1)
    emb = (emb_ln_g.reshape(1, d), emb_ln_b.reshape(1, d))

    h = x
    for i in range(depth):
        h = _mixer_call(h.reshape(bsz, seq, d), wm, i, alpha, emb=emb if i == 0 else None)
        h = _ffn_call(h.reshape(bsz * seq, d), p3, wf, i, alpha)
    return h.reshape(bsz, seq, d)
```
